```python
import math
import jax, jax.numpy as jnp
from jax import lax
import numpy as np

D_MODEL = 4096
BATCH = 4
SEQ = 2048
DEPTH = 4
DEC_BATCH = 8
DEC_SEQ = 16
PAST_LEN = 2048

CHUNK = 64
EPS = 1e-6
N_AB_LAYERS = (DEPTH + 1) // 2
N_C_LAYERS = DEPTH // 2
QUERY_BLOCK = 128
MLA_HEADS = 16
MLA_NOPE = 128
MLA_ROPE = 64
MLA_QK = MLA_NOPE + MLA_ROPE
MLA_V = 128
Q_RANK = 768
KV_RANK = 512
ROPE_THETA = 10000.0
A_WIDTH = MLA_HEADS * MLA_V
B_WIDTH = D_MODEL - A_WIDTH
SSM_GROUP = 16
SSM_GROUPS = B_WIDTH // SSM_GROUP
SSM_STATE = 64
DT_MIN = 1e-3
DT_MAX = 1e-1
SWA_HEADS = 64
SWA_KV_HEADS = 8
SWA_GQ = SWA_HEADS // SWA_KV_HEADS
SWA_HEAD_DIM = D_MODEL // SWA_HEADS
C_WIDTH = SWA_HEADS * SWA_HEAD_DIM
SWA_KV_DIM = SWA_KV_HEADS * SWA_HEAD_DIM
WINDOW = 128
WIN_CHUNKS = WINDOW // CHUNK
REL_BUCKETS = 32
REL_MAX_DIST = 128
AB_IN = Q_RANK + KV_RANK + MLA_ROPE + A_WIDTH + B_WIDTH + B_WIDTH
AB_SPLITS = (Q_RANK, Q_RANK + KV_RANK, Q_RANK + KV_RANK + MLA_ROPE,
             Q_RANK + KV_RANK + MLA_ROPE + A_WIDTH,
             Q_RANK + KV_RANK + MLA_ROPE + A_WIDTH + B_WIDTH)
C_IN = C_WIDTH + 2 * SWA_KV_DIM + C_WIDTH
C_SPLITS = (C_WIDTH, C_WIDTH + SWA_KV_DIM, C_WIDTH + 2 * SWA_KV_DIM)

kernel_name = "hybrid_streaming_mla_s5_swa_step"


def rms_norm(x, g):
    xf = x.astype(jnp.float32)
    y = xf * lax.rsqrt(jnp.mean(xf * xf, axis=-1, keepdims=True) + EPS)
    return (y * g.astype(jnp.float32)).astype(x.dtype)


def rope(x, pos):
    half = x.shape[-1] // 2
    inv = ROPE_THETA ** (-jnp.arange(half, dtype=jnp.float32) / half)
    ang = pos.astype(jnp.float32)[:, None, None] * inv
    cos, sin = jnp.cos(ang), jnp.sin(ang)
    xf = x.astype(jnp.float32)
    x1, x2 = xf[..., :half], xf[..., half:]
    return jnp.concatenate([x1 * cos - x2 * sin, x1 * sin + x2 * cos], axis=-1).astype(x.dtype)


def chunk_causal_mask(q_pos, k_pos):
    return (k_pos[None, :] // CHUNK) <= (q_pos[:, None] // CHUNK)


def rel_bucket(rel):
    nb = REL_BUCKETS // 2
    max_exact = nb // 2
    ret = jnp.where(rel > 0, nb, 0)
    n = jnp.abs(rel)
    nf = jnp.maximum(n, 1).astype(jnp.float32)
    large = max_exact + (jnp.log(nf / max_exact) / math.log(REL_MAX_DIST / max_exact)
                         * (nb - max_exact)).astype(jnp.int32)
    large = jnp.minimum(large, nb - 1)
    return ret + jnp.where(n < max_exact, n, large)


def rel_bias(table, rel):
    b = jnp.take(table, rel_bucket(rel), axis=0).astype(jnp.float32)
    return jnp.transpose(b, (2, 0, 1)).reshape(SWA_KV_HEADS, SWA_GQ, rel.shape[0], rel.shape[1])


def mla_attention(q, k, v, mask):
    s = jnp.einsum('bqhd,bkhd->bhqk', q, k, preferred_element_type=jnp.float32) * (MLA_QK ** -0.5)
    s = jnp.where(mask, s, -1e30)
    p = jax.nn.softmax(s, axis=-1)
    return jnp.einsum('bhqk,bkhd->bqhd', p.astype(v.dtype), v)


def mla_prompt_attention(q, k, v):
    bsz, L = q.shape[:2]
    nqb = L // QUERY_BLOCK
    qb = jnp.moveaxis(q.reshape(bsz, nqb, QUERY_BLOCK, MLA_HEADS, MLA_QK), 1, 0)
    k_pos = jnp.arange(L)

    def one_block(args):
        q_blk, start = args
        mask = chunk_causal_mask(start + jnp.arange(QUERY_BLOCK), k_pos)
        return mla_attention(q_blk, k, v, mask)

    o = lax.map(one_block, (qb, jnp.arange(nqb) * QUERY_BLOCK))
    return jnp.moveaxis(o, 0, 1).reshape(bsz, L, MLA_HEADS, MLA_V)


def mla_keys(lat, k_rope, w_ukv, gk):
    bsz, T = lat.shape[:2]
    kv = (lat @ w_ukv).reshape(bsz, T, MLA_HEADS, MLA_NOPE + MLA_V)
    k_pe = jnp.broadcast_to(k_rope[:, :, None, :], (bsz, T, MLA_HEADS, MLA_ROPE))
    k = rms_norm(jnp.concatenate([kv[..., :MLA_NOPE], k_pe], axis=-1), gk)
    return k, kv[..., MLA_NOPE:]


def s5_discretize(lam_re, lam_im, b_re, b_im, log_dt):
    lam_re = jnp.minimum(lam_re.astype(jnp.float32), -1e-4)
    lam_im = lam_im.astype(jnp.float32)
    dt = jnp.exp(log_dt.astype(jnp.float32))[:, None]
    mag = jnp.exp(lam_re * dt)
    lb_re, lb_im = mag * jnp.cos(lam_im * dt), mag * jnp.sin(lam_im * dt)
    nr, ni = lb_re - 1.0, lb_im
    den = lam_re * lam_re + lam_im * lam_im
    f_re = (nr * lam_re + ni * lam_im) / den
    f_im = (ni * lam_re - nr * lam_im) / den
    b_re, b_im = b_re.astype(jnp.float32), b_im.astype(jnp.float32)
    bb_re = f_re[..., None] * b_re - f_im[..., None] * b_im
    bb_im = f_re[..., None] * b_im + f_im[..., None] * b_re
    return lb_re, lb_im, bb_re, bb_im


def s5_combine(e1, e2):
    a1r, a1i, b1r, b1i = e1
    a2r, a2i, b2r, b2i = e2
    return (a2r * a1r - a2i * a1i, a2r * a1i + a2i * a1r,
            a2r * b1r - a2i * b1i + b2r, a2r * b1i + a2i * b1r + b2i)


def s5_block(u, h0_re, h0_im, disc):
    lb_re, lb_im, bb_re, bb_im = disc
    bu_re = jnp.einsum('btgc,gpc->btgp', u, bb_re)
    bu_im = jnp.einsum('btgc,gpc->btgp', u, bb_im)
    bu_re = bu_re.at[:, 0].add(lb_re * h0_re - lb_im * h0_im)
    bu_im = bu_im.at[:, 0].add(lb_re * h0_im + lb_im * h0_re)
    a_re = jnp.broadcast_to(lb_re, bu_re.shape)
    a_im = jnp.broadcast_to(lb_im, bu_re.shape)
    _, _, hr, hi = lax.associative_scan(s5_combine, (a_re, a_im, bu_re, bu_im), axis=1)
    return hr, hi


def s5_readout(hr, hi, c_re, c_im):
    return (jnp.einsum('btgp,gcp->btgc', hr, c_re.astype(jnp.float32))
            - jnp.einsum('btgp,gcp->btgc', hi, c_im.astype(jnp.float32)))


def s5_prompt(u, disc, c_re, c_im):
    bsz, L = u.shape[:2]
    nc = L // CHUNK
    uc = jnp.moveaxis(u.reshape(bsz, nc, CHUNK, SSM_GROUPS, SSM_GROUP), 1, 0)
    h0 = jnp.zeros((bsz, SSM_GROUPS, SSM_STATE), jnp.float32)

    def step(carry, u_blk):
        hr, hi = s5_block(u_blk, carry[0], carry[1], disc)
        return (hr[:, -1], hi[:, -1]), s5_readout(hr, hi, c_re, c_im)

    (hr, hi), ys = lax.scan(step, (h0, h0), uc)
    return jnp.moveaxis(ys, 0, 1).reshape(bsz, L, SSM_GROUPS, SSM_GROUP), hr, hi


def ab_layer(x, past, params):
    (norm_g, w_in, q_lora_g, kv_lora_g, w_uq, w_ukv, gq, gk, lam_re, lam_im,
     b_re, b_im, c_re, c_im, log_dt, d_skip, w_glu, w_out) = params
    bsz, T, _ = x.shape
    h = rms_norm(x, norm_g)
    c_q, c_kv, k_rope, gate_a, u, gate_b = jnp.split(h @ w_in, AB_SPLITS, axis=-1)
    c_q = rms_norm(c_q, q_lora_g)
    c_kv = rms_norm(c_kv, kv_lora_g)
    past_len = 0 if past is None else past[0].shape[1]
    pos = past_len + jnp.arange(T)
    q = (c_q @ w_uq).reshape(bsz, T, MLA_HEADS, MLA_QK)
    q = rms_norm(jnp.concatenate([q[..., :MLA_NOPE], rope(q[..., MLA_NOPE:], pos)], axis=-1), gq)
    k_rope = rope(k_rope[:, :, None, :], pos)[:, :, 0]
    uf = u.astype(jnp.float32).reshape(bsz, T, SSM_GROUPS, SSM_GROUP)
    disc = s5_discretize(lam_re, lam_im, b_re, b_im, log_dt)
    if past is None:
        k, v = mla_keys(c_kv, k_rope, w_ukv, gk)
        o = mla_prompt_attention(q, k, v)
        y, h_re, h_im = s5_prompt(uf, disc, c_re, c_im)
    else:
        lat_c, kr_c, h0_re, h0_im = past
        k, v = mla_keys(jnp.concatenate([lat_c, c_kv], axis=1),
                        jnp.concatenate([kr_c, k_rope], axis=1), w_ukv, gk)
        o = mla_attention(q, k, v, chunk_causal_mask(pos, jnp.arange(past_len + T)))
        hr, hi = s5_block(uf, h0_re, h0_im, disc)
        y = s5_readout(hr, hi, c_re, c_im)
        h_re, h_im = hr[:, -1], hi[:, -1]
    a_out = o.reshape(bsz, T, A_WIDTH) * jax.nn.silu(gate_a)
    y = y.reshape(bsz, T, B_WIDTH) + d_skip.astype(jnp.float32) * uf.reshape(bsz, T, B_WIDTH)
    y = jax.nn.gelu(y).astype(x.dtype)
    g_a, g_b = jnp.split(y @ w_glu, 2, axis=-1)
    b_out = g_a * jax.nn.sigmoid(g_b) * jax.nn.silu(gate_b)
    out = jnp.concatenate([a_out, b_out], axis=-1) @ w_out
    return x + out, (c_kv, k_rope, h_re, h_im)


def sink_attention(q, k, v, bias, mask, sinks):
    s = jnp.einsum('...qhgd,...khd->...hgqk', q, k, preferred_element_type=jnp.float32) * (SWA_HEAD_DIM ** -0.5)
    s = jnp.where(mask, s + bias, -1e30)
    sink = sinks.astype(jnp.float32).reshape(SWA_KV_HEADS, SWA_GQ, 1, 1)
    m = jnp.maximum(jnp.max(s, axis=-1, keepdims=True), sink)
    e = jnp.exp(s - m)
    p = e / (jnp.sum(e, axis=-1, keepdims=True) + jnp.exp(sink - m))
    return jnp.einsum('...hgqk,...khd->...qhgd', p.astype(v.dtype), v)


def swa_prompt(q, k, v, table, sinks):
    bsz, L = q.shape[:2]
    nc = L // CHUNK
    qc = q.reshape(bsz, nc, CHUNK, SWA_KV_HEADS, SWA_GQ, SWA_HEAD_DIM)

    def windows(t):
        tp = jnp.pad(t, ((0, 0), (WINDOW, 0), (0, 0), (0, 0)))
        tp = tp.reshape(bsz, nc + WIN_CHUNKS, CHUNK, SWA_KV_HEADS, SWA_HEAD_DIM)
        return jnp.concatenate([tp[:, j:j + nc] for j in range(WIN_CHUNKS + 1)], axis=2)

    kw, vw = windows(k), windows(v)
    qi = jnp.arange(CHUNK)
    kj = jnp.arange(WINDOW + CHUNK)
    bias = rel_bias(table, kj[None, :] - WINDOW - qi[:, None])
    valid = (jnp.arange(nc)[:, None] * CHUNK - WINDOW + kj[None, :]) >= 0
    o = sink_attention(qc, kw, vw, bias, valid[None, :, None, None, None, :], sinks)
    return o.reshape(bsz, L, C_WIDTH)


def c_layer(x, past, past_len, params, table):
    norm_g, w_in, gq, gk, sinks, w_out = params
    bsz, T, _ = x.shape
    h = rms_norm(x, norm_g)
    q, k, v, gate = jnp.split(h @ w_in, C_SPLITS, axis=-1)
    q = rms_norm(q.reshape(bsz, T, SWA_KV_HEADS, SWA_GQ, SWA_HEAD_DIM), gq)
    k = rms_norm(k.reshape(bsz, T, SWA_KV_HEADS, SWA_HEAD_DIM), gk)
    v = v.reshape(bsz, T, SWA_KV_HEADS, SWA_HEAD_DIM)
    if past is None:
        o = swa_prompt(q, k, v, table, sinks)
        k_buf, v_buf = k[:, -WINDOW:], v[:, -WINDOW:]
    else:
        kc = jnp.concatenate([past[0], k], axis=1)
        vc = jnp.concatenate([past[1], v], axis=1)
        q_pos = past_len + jnp.arange(T)
        k_pos = past_len - WINDOW + jnp.arange(WINDOW + T)
        qch, kch = q_pos[:, None] // CHUNK, k_pos[None, :] // CHUNK
        mask = (kch <= qch) & (kch >= qch - WIN_CHUNKS)
        bias = rel_bias(table, k_pos[None, :] - q_pos[:, None])
        o = sink_attention(q, kc, vc, bias, mask, sinks).reshape(bsz, T, C_WIDTH)
        k_buf, v_buf = kc[:, -WINDOW:], vc[:, -WINDOW:]
    out = (o * jax.nn.silu(gate)) @ w_out
    return x + out, (k_buf, v_buf)


def setup_inputs(seed: int = 0) -> dict:
    key = jax.random.key(seed)
    keys = jax.random.split(key, 40)

    def nrm(i, shape, scale=1.0):
        return scale * jax.random.normal(keys[i], shape, jnp.float32)

    def gain(i, shape):
        return 1.0 + nrm(i, shape, 0.01)

    n = jnp.arange(SSM_STATE, dtype=jnp.float32)
    ssm_shape = (N_AB_LAYERS, SSM_GROUPS, SSM_STATE)
    return {
        "x_prompt": nrm(0, (BATCH, SEQ, D_MODEL)),
        "x_sample": nrm(1, (DEC_BATCH, DEC_SEQ, D_MODEL)),
        "cache_mla_latent": nrm(2, (N_AB_LAYERS, DEC_BATCH, PAST_LEN, KV_RANK)),
        "cache_mla_krope": nrm(3, (N_AB_LAYERS, DEC_BATCH, PAST_LEN, MLA_ROPE)),
        "state_ssm_re": nrm(4, (N_AB_LAYERS, DEC_BATCH, SSM_GROUPS, SSM_STATE), 0.3),
        "state_ssm_im": nrm(5, (N_AB_LAYERS, DEC_BATCH, SSM_GROUPS, SSM_STATE), 0.3),
        "cache_swa_k": nrm(6, (N_C_LAYERS, DEC_BATCH, WINDOW, SWA_KV_HEADS, SWA_HEAD_DIM)),
        "cache_swa_v": nrm(7, (N_C_LAYERS, DEC_BATCH, WINDOW, SWA_KV_HEADS, SWA_HEAD_DIM)),
        "rel_bias_table": nrm(8, (REL_BUCKETS, SWA_HEADS), 0.5),
        "ab_norm": gain(9, (N_AB_LAYERS, D_MODEL)),
        "ab_w_in": nrm(10, (N_AB_LAYERS, D_MODEL, AB_IN), D_MODEL ** -0.5),
        "ab_q_lora_norm": gain(11, (N_AB_LAYERS, Q_RANK)),
        "ab_kv_lora_norm": gain(12, (N_AB_LAYERS, KV_RANK)),
        "ab_w_uq": nrm(13, (N_AB_LAYERS, Q_RANK, MLA_HEADS * MLA_QK), Q_RANK ** -0.5),
        "ab_w_ukv": nrm(14, (N_AB_LAYERS, KV_RANK, MLA_HEADS * (MLA_NOPE + MLA_V)), KV_RANK ** -0.5),
        "ab_q_norm": gain(15, (N_AB_LAYERS, MLA_QK)),
        "ab_k_norm": gain(16, (N_AB_LAYERS, MLA_QK)),
        "ssm_lambda_re": -0.5 + nrm(17, ssm_shape, 0.01),
        "ssm_lambda_im": math.pi * n + nrm(18, ssm_shape, 0.01),
        "ssm_b_re": nrm(19, (N_AB_LAYERS, SSM_GROUPS, SSM_STATE, SSM_GROUP), (2 * SSM_GROUP) ** -0.5),
        "ssm_b_im": nrm(20, (N_AB_LAYERS, SSM_GROUPS, SSM_STATE, SSM_GROUP), (2 * SSM_GROUP) ** -0.5),
        "ssm_c_re": nrm(21, (N_AB_LAYERS, SSM_GROUPS, SSM_GROUP, SSM_STATE), SSM_STATE ** -0.5),
        "ssm_c_im": nrm(22, (N_AB_LAYERS, SSM_GROUPS, SSM_GROUP, SSM_STATE), SSM_STATE ** -0.5),
        "ssm_log_dt": jax.random.uniform(keys[23], (N_AB_LAYERS, SSM_GROUPS), jnp.float32,
                                         minval=math.log(DT_MIN), maxval=math.log(DT_MAX)),
        "ssm_d": nrm(24, (N_AB_LAYERS, B_WIDTH)),
        "ssm_w_glu": nrm(25, (N_AB_LAYERS, B_WIDTH, 2 * B_WIDTH), B_WIDTH ** -0.5),
        "ab_w_out": nrm(26, (N_AB_LAYERS, A_WIDTH + B_WIDTH, D_MODEL), D_MODEL ** -0.5),
        "c_norm": gain(27, (N_C_LAYERS, D_MODEL)),
        "c_w_in": nrm(28, (N_C_LAYERS, D_MODEL, C_IN), D_MODEL ** -0.5),
        "c_q_norm": gain(29, (N_C_LAYERS, SWA_HEAD_DIM)),
        "c_k_norm": gain(30, (N_C_LAYERS, SWA_HEAD_DIM)),
        "c_sinks": nrm(31, (N_C_LAYERS, SWA_HEADS), 0.5),
        "c_w_out": nrm(32, (N_C_LAYERS, C_WIDTH, D_MODEL), C_WIDTH ** -0.5),
    }


def reference(x_prompt, x_sample, cache_mla_latent, cache_mla_krope, state_ssm_re, state_ssm_im,
              cache_swa_k, cache_swa_v, rel_bias_table, ab_norm, ab_w_in, ab_q_lora_norm,
              ab_kv_lora_norm, ab_w_uq, ab_w_ukv, ab_q_norm, ab_k_norm, ssm_lambda_re, ssm_lambda_im,
              ssm_b_re, ssm_b_im, ssm_c_re, ssm_c_im, ssm_log_dt, ssm_d, ssm_w_glu, ab_w_out,
              c_norm, c_w_in, c_q_norm, c_k_norm, c_sinks, c_w_out):
    xp, xs = x_prompt, x_sample
    past_len = cache_mla_latent.shape[2]
    lat_p, kr_p, sre_p, sim_p, k_p, v_p = [], [], [], [], [], []
    lat_s, kr_s, sre_s, sim_s, k_s, v_s = [], [], [], [], [], []
    for layer in range(DEPTH):
        i = layer // 2
        if layer % 2 == 0:
            prm = (ab_norm[i], ab_w_in[i], ab_q_lora_norm[i], ab_kv_lora_norm[i], ab_w_uq[i], ab_w_ukv[i],
                   ab_q_norm[i], ab_k_norm[i], ssm_lambda_re[i], ssm_lambda_im[i], ssm_b_re[i], ssm_b_im[i],
                   ssm_c_re[i], ssm_c_im[i], ssm_log_dt[i], ssm_d[i], ssm_w_glu[i], ab_w_out[i])
            xp, (a, b, c, d) = ab_layer(xp, None, prm)
            lat_p.append(a); kr_p.append(b); sre_p.append(c); sim_p.append(d)
            past = (cache_mla_latent[i], cache_mla_krope[i], state_ssm_re[i], state_ssm_im[i])
            xs, (a, b, c, d) = ab_layer(xs, past, prm)
            lat_s.append(a); kr_s.append(b); sre_s.append(c); sim_s.append(d)
        else:
            prm = (c_norm[i], c_w_in[i], c_q_norm[i], c_k_norm[i], c_sinks[i], c_w_out[i])
            xp, (a, b) = c_layer(xp, None, 0, prm, rel_bias_table)
            k_p.append(a); v_p.append(b)
            xs, (a, b) = c_layer(xs, (cache_swa_k[i], cache_swa_v[i]), past_len, prm, rel_bias_table)
            k_s.append(a); v_s.append(b)
    return (xp, xs,
            jnp.stack(lat_p), jnp.stack(kr_p), jnp.stack(sre_p), jnp.stack(sim_p),
            jnp.stack(k_p), jnp.stack(v_p),
            jnp.stack(lat_s), jnp.stack(kr_s), jnp.stack(sre_s), jnp.stack(sim_s),
            jnp.stack(k_s), jnp.stack(v_s))
```

```python
import functools
import math

import jax
import jax.numpy as jnp
import numpy as np
from jax import lax
from jax.experimental import pallas as pl
from jax.experimental.pallas import tpu as pltpu

F32 = jnp.float32
BF16 = jnp.bfloat16

EPS = 1e-6
CHUNK = 64
ROPE_THETA = 10000.0
MLA_HEADS = 16
MLA_NOPE = 128
MLA_ROPE = 64
MLA_QK = MLA_NOPE + MLA_ROPE
MLA_V = 128
MLA_PAD = 256
SSM_GROUP = 16
SSM_STATE = 64
SSM_SUB = 8
SWA_HEADS = 64
SWA_KV_HEADS = 8
SWA_GQ = SWA_HEADS // SWA_KV_HEADS
SWA_HEAD_DIM = 64
WINDOW = 128
REL_BUCKETS = 32
REL_MAX_DIST = 128
NEG = -1e30
LANE = 128
VMEM_LIMIT = 56 * 1024 * 1024

AB_CQ = 0
AB_KR = 768
AB_CKV = 1024
AB_GA = 1536
AB_U = 3584
AB_GB = 5632
AB_COLS = 7680


def _cparams(*sem):
    return pltpu.CompilerParams(dimension_semantics=sem, vmem_limit_bytes=VMEM_LIMIT)


def _rms(x, g):
    return x * lax.rsqrt(jnp.mean(x * x, axis=-1, keepdims=True) + EPS) * g


def _silu(x):
    return x * jax.nn.sigmoid(x)


def _gelu_tanh(x):
    return 0.5 * x * (1.0 + jnp.tanh(math.sqrt(2.0 / math.pi) * (x + 0.044715 * (x * x * x))))


def _pick(n, prefs):
    for p in prefs:
        if n % p == 0:
            return p
    raise ValueError(f"no tile for {n} in {prefs}")


def _norm_mm_kernel(x_ref, g_ref, w_ref, o_ref, h_ref):
    @pl.when(pl.program_id(1) == 0)
    def _():
        h_ref[...] = _rms(x_ref[...], g_ref[...]).astype(BF16)

    o_ref[...] = jnp.dot(h_ref[...], w_ref[...], preferred_element_type=F32).astype(o_ref.dtype)


def norm_matmul(x, g, w, *, tn, out_dtype=F32):
    m, k = x.shape
    n = w.shape[1]
    tm = _pick(m, (512, 256, 128))
    return pl.pallas_call(
        _norm_mm_kernel,
        grid=(m // tm, n // tn),
        in_specs=[pl.BlockSpec((tm, k), lambda i, j: (i, 0)),
                  pl.BlockSpec((1, k), lambda i, j: (0, 0)),
                  pl.BlockSpec((k, tn), lambda i, j: (0, j))],
        out_specs=pl.BlockSpec((tm, tn), lambda i, j: (i, j)),
        out_shape=jax.ShapeDtypeStruct((m, n), out_dtype),
        scratch_shapes=[pltpu.VMEM((tm, k), BF16)],
        compiler_params=_cparams("parallel", "arbitrary"),
    )(x, g.reshape(1, k), w)


def _out_proj_kernel(*refs, n_lhs):
    lhs = refs[:n_lhs]
    w_ref, x_ref, o_ref = refs[n_lhs:]
    acc = x_ref[...]
    off = 0
    for a in lhs:
        kk = a.shape[1]
        acc = acc + jnp.dot(a[...], w_ref[off:off + kk, :], preferred_element_type=F32)
        off += kk
    o_ref[...] = acc


def out_proj(lhs, w, x, *, tn=512):
    m, n = x.shape
    k = w.shape[0]
    tm = _pick(m, (1024, 512, 256, 128))
    in_specs = [pl.BlockSpec((tm, a.shape[1]), lambda i, j: (i, 0)) for a in lhs]
    in_specs += [pl.BlockSpec((k, tn), lambda i, j: (0, j)),
                 pl.BlockSpec((tm, tn), lambda i, j: (i, j))]
    return pl.pallas_call(
        functools.partial(_out_proj_kernel, n_lhs=len(lhs)),
        grid=(m // tm, n // tn),
        in_specs=in_specs,
        out_specs=pl.BlockSpec((tm, tn), lambda i, j: (i, j)),
        out_shape=jax.ShapeDtypeStruct((m, n), F32),
        compiler_params=_cparams("parallel", "arbitrary"),
    )(*lhs, w, x)


def _rope_slab(r, c, s1, s2):
    return r * c + pltpu.roll(r, 96, 1) * s1 + pltpu.roll(r, 32, 1) * s2


def _q_prep_kernel(cq_ref, g_ref, w_ref, c_ref, s1_ref, s2_ref, gn_ref, gr_ref, q_ref):
    h = _rms(cq_ref[...], g_ref[...]).astype(BF16)
    c, s1, s2 = c_ref[...], s1_ref[...], s2_ref[...]
    scale = MLA_QK ** -0.5
    gn = gn_ref[...] * scale
    gr = gr_ref[...] * scale
    for hd in range(MLA_HEADS):
        lo = hd * MLA_PAD
        nope = jnp.dot(h, w_ref[:, lo:lo + LANE], preferred_element_type=F32)
        rr = _rope_slab(jnp.dot(h, w_ref[:, lo + LANE:lo + MLA_PAD], preferred_element_type=F32), c, s1, s2)
        ss = jnp.sum(nope * nope, axis=-1, keepdims=True) + jnp.sum(rr * rr, axis=-1, keepdims=True)
        rs = lax.rsqrt(ss * (1.0 / MLA_QK) + EPS)
        q_ref[:, lo:lo + LANE] = (nope * rs * gn).astype(BF16)
        q_ref[:, lo + LANE:lo + MLA_PAD] = (rr * rs * gr).astype(BF16)


def mla_q_prep(p, g, w_uq, tabs, gn, gr):
    m = p.shape[0]
    period = tabs[0].shape[0]
    tm = _pick(period, (256, 128))
    nb = period // tm
    tab_spec = pl.BlockSpec((tm, LANE), lambda i: (i % nb, 0))
    vec = lambda n: pl.BlockSpec((1, n), lambda i: (0, 0))
    return pl.pallas_call(
        _q_prep_kernel,
        grid=(m // tm,),
        in_specs=[pl.BlockSpec((tm, 768), lambda i: (i, 0)), vec(768),
                  pl.BlockSpec(w_uq.shape, lambda i: (0, 0)),
                  tab_spec, tab_spec, tab_spec, vec(LANE), vec(LANE)],
        out_specs=pl.BlockSpec((tm, MLA_HEADS * MLA_PAD), lambda i: (i, 0)),
        out_shape=jax.ShapeDtypeStruct((m, MLA_HEADS * MLA_PAD), BF16),
        compiler_params=_cparams("parallel"),
    )(p, g.reshape(1, -1), w_uq, *tabs, gn, gr)


def _latent_kernel(ckv_ref, kr_ref, g_ref, c_ref, s1_ref, s2_ref, lat_ref, krp_ref, kro_ref):
    lat_ref[...] = _rms(ckv_ref[...], g_ref[...])
    rr = _rope_slab(kr_ref[...], c_ref[...], s1_ref[...], s2_ref[...])
    krp_ref[...] = rr
    kro_ref[...] = rr[:, :MLA_ROPE]


def mla_latent(p, g, tabs):
    m = p.shape[0]
    period = tabs[0].shape[0]
    tm = _pick(period, (512, 256, 128))
    nb = period // tm
    tab_spec = pl.BlockSpec((tm, LANE), lambda i: (i % nb, 0))
    return pl.pallas_call(
        _latent_kernel,
        grid=(m // tm,),
        in_specs=[pl.BlockSpec((tm, 512), lambda i: (i, AB_CKV // 512)),
                  pl.BlockSpec((tm, LANE), lambda i: (i, AB_KR // LANE)),
                  pl.BlockSpec((1, 512), lambda i: (0, 0)),
                  tab_spec, tab_spec, tab_spec],
        out_specs=[pl.BlockSpec((tm, 512), lambda i: (i, 0)),
                   pl.BlockSpec((tm, LANE), lambda i: (i, 0)),
                   pl.BlockSpec((tm, MLA_ROPE), lambda i: (i, 0))],
        out_shape=[jax.ShapeDtypeStruct((m, 512), F32),
                   jax.ShapeDtypeStruct((m, LANE), F32),
                   jax.ShapeDtypeStruct((m, MLA_ROPE), F32)],
        compiler_params=_cparams("parallel"),
    )(p, p, g.reshape(1, -1), *tabs)


def _kv_expand_kernel(lat_ref, krp_ref, wn_ref, wv_ref, gn_ref, gr_ref, k_ref, v_ref):
    lat = lat_ref[...].astype(BF16)
    kr = krp_ref[...]
    ss_r = jnp.sum(kr * kr, axis=-1, keepdims=True)
    gn, gr = gn_ref[...], gr_ref[...]
    for hd in range(MLA_HEADS):
        kn = jnp.dot(lat, wn_ref[:, hd * LANE:(hd + 1) * LANE], preferred_element_type=F32)
        rs = lax.rsqrt((jnp.sum(kn * kn, axis=-1, keepdims=True) + ss_r) * (1.0 / MLA_QK) + EPS)
        lo = hd * MLA_PAD
        k_ref[:, lo:lo + LANE] = (kn * rs * gn).astype(BF16)
        k_ref[:, lo + LANE:lo + MLA_PAD] = (kr * rs * gr).astype(BF16)
        v_ref[:, hd * LANE:(hd + 1) * LANE] = jnp.dot(
            lat, wv_ref[:, hd * LANE:(hd + 1) * LANE], preferred_element_type=F32).astype(BF16)


def mla_kv_expand(lat, krp, wn, wv, gn, gr):
    m = lat.shape[0]
    tm = _pick(m, (512, 256, 128))
    vec = pl.BlockSpec((1, LANE), lambda i: (0, 0))
    return pl.pallas_call(
        _kv_expand_kernel,
        grid=(m // tm,),
        in_specs=[pl.BlockSpec((tm, 512), lambda i: (i, 0)),
                  pl.BlockSpec((tm, LANE), lambda i: (i, 0)),
                  pl.BlockSpec(wn.shape, lambda i: (0, 0)),
                  pl.BlockSpec(wv.shape, lambda i: (0, 0)), vec, vec],
        out_specs=[pl.BlockSpec((tm, MLA_HEADS * MLA_PAD), lambda i: (i, 0)),
                   pl.BlockSpec((tm, MLA_HEADS * MLA_V), lambda i: (i, 0))],
        out_shape=[jax.ShapeDtypeStruct((m, MLA_HEADS * MLA_PAD), BF16),
                   jax.ShapeDtypeStruct((m, MLA_HEADS * MLA_V), BF16)],
        compiler_params=_cparams("parallel"),
    )(lat, krp, wn, wv, gn, gr)


def _softmax_step(q, kblk, vblk, carry, mask=None):
    m, l, acc = carry
    s = lax.dot_general(q, kblk, (((1,), (1,)), ((), ())), preferred_element_type=F32)
    if mask is not None:
        s = jnp.where(mask, s, NEG)
    m_new = jnp.maximum(m, jnp.max(s, axis=-1, keepdims=True))
    p = jnp.exp(s - m_new)
    alpha = jnp.exp(m - m_new)
    l = alpha * l + jnp.sum(p, axis=-1, keepdims=True)
    acc = alpha * acc + jnp.dot(p.astype(BF16), vblk, preferred_element_type=F32)
    return m_new, l, acc


def _mla_attn_kernel(q_ref, k_ref, v_ref, g_ref, o_ref, *, tq):
    i = pl.program_id(2)
    q = q_ref[...]

    def body(j, carry):
        off = pl.multiple_of(j * tq, tq)
        return _softmax_step(q, k_ref[pl.ds(off, tq), :], v_ref[pl.ds(off, tq), :], carry)

    init = (jnp.full((tq, 1), NEG, F32), jnp.zeros((tq, 1), F32), jnp.zeros((tq, MLA_V), F32))
    carry = lax.fori_loop(0, i, body, init)
    off = pl.multiple_of(i * tq, tq)
    row = lax.broadcasted_iota(jnp.int32, (tq, tq), 0) // CHUNK
    col = lax.broadcasted_iota(jnp.int32, (tq, tq), 1) // CHUNK
    _, l, acc = _softmax_step(q, k_ref[pl.ds(off, tq), :], v_ref[pl.ds(off, tq), :], carry, col <= row)
    o_ref[...] = (acc / l * _silu(g_ref[...])).astype(o_ref.dtype)


def mla_attn_prompt(q, k, v, p, *, bsz, seq):
    tq = 256
    nq = seq // tq
    return pl.pallas_call(
        functools.partial(_mla_attn_kernel, tq=tq),
        grid=(bsz, MLA_HEADS, nq),
        in_specs=[pl.BlockSpec((tq, MLA_PAD), lambda b, h, i: (b * nq + i, h)),
                  pl.BlockSpec((seq, MLA_PAD), lambda b, h, i: (b, h)),
                  pl.BlockSpec((seq, MLA_V), lambda b, h, i: (b, h)),
                  pl.BlockSpec((tq, MLA_V), lambda b, h, i: (b * nq + i, AB_GA // MLA_V + h))],
        out_specs=pl.BlockSpec((tq, MLA_V), lambda b, h, i: (b * nq + i, h)),
        out_shape=jax.ShapeDtypeStruct((bsz * seq, MLA_HEADS * MLA_V), BF16),
        compiler_params=_cparams("parallel", "parallel", "arbitrary"),
    )(q, k, v, p)


def _mla_attn_step_kernel(q_ref, kc_ref, vc_ref, kn_ref, vn_ref, g_ref, o_ref):
    q = q_ref[...]
    t = q.shape[0]
    init = (jnp.full((t, 1), NEG, F32), jnp.zeros((t, 1), F32), jnp.zeros((t, MLA_V), F32))
    carry = _softmax_step(q, kc_ref[...], vc_ref[...], init)
    _, l, acc = _softmax_step(q, kn_ref[...], vn_ref[...], carry)
    o_ref[...] = (acc / l * _silu(g_ref[...])).astype(o_ref.dtype)


def mla_attn_step(q, kc, vc, kn, vn, p, *, bsz, t, past):
    return pl.pallas_call(
        _mla_attn_step_kernel,
        grid=(bsz, MLA_HEADS),
        in_specs=[pl.BlockSpec((t, MLA_PAD), lambda b, h: (b, h)),
                  pl.BlockSpec((past, MLA_PAD), lambda b, h: (b, h)),
                  pl.BlockSpec((past, MLA_V), lambda b, h: (b, h)),
                  pl.BlockSpec((t, MLA_PAD), lambda b, h: (b, h)),
                  pl.BlockSpec((t, MLA_V), lambda b, h: (b, h)),
                  pl.BlockSpec((t, MLA_V), lambda b, h: (b, AB_GA // MLA_V + h))],
        out_specs=pl.BlockSpec((t, MLA_V), lambda b, h: (b, h)),
        out_shape=jax.ShapeDtypeStruct((bsz * t, MLA_HEADS * MLA_V), BF16),
        compiler_params=_cparams("parallel", "parallel"),
    )(q, kc, vc, kn, vn, p)


def _s5_disc_kernel(lre_ref, lim_ref, ldt_ref, bre_ref, bim_ref, lbre_ref, lbim_ref, bbre_ref, bbim_ref):
    lam_re = jnp.minimum(lre_ref[...], -1e-4)
    lam_im = lim_ref[...]
    dt = jnp.exp(ldt_ref[...])
    mag = jnp.exp(lam_re * dt)
    lb_re = mag * jnp.cos(lam_im * dt)
    lb_im = mag * jnp.sin(lam_im * dt)
    nr, ni = lb_re - 1.0, lb_im
    den = lam_re * lam_re + lam_im * lam_im
    f_re = (nr * lam_re + ni * lam_im) / den
    f_im = (ni * lam_re - nr * lam_im) / den
    b_re, b_im = bre_ref[...], bim_ref[...]
    lbre_ref[...] = lb_re
    lbim_ref[...] = lb_im
    bbre_ref[...] = f_re * b_re - f_im * b_im
    bbim_ref[...] = f_re * b_im + f_im * b_re


def s5_discretize(lam_re, lam_im, log_dt, b_re, b_im):
    g, pdim = lam_re.shape
    bt_re = jnp.swapaxes(b_re, 1, 2)
    bt_im = jnp.swapaxes(b_im, 1, 2)
    s3 = jax.ShapeDtypeStruct((g, 1, pdim), F32)
    sb = jax.ShapeDtypeStruct(bt_re.shape, F32)
    return pl.pallas_call(
        _s5_disc_kernel, out_shape=[s3, s3, sb, sb],
    )(lam_re.reshape(g, 1, pdim), lam_im.reshape(g, 1, pdim), log_dt.reshape(g, 1, 1), bt_re, bt_im)


def _s5_kernel(u_ref, h0re_ref, h0im_ref, wbu_ref, lbre_ref, lbim_ref, wcre_ref, wcim_ref, d_ref,
               y_ref, hre_ref, him_ref, sre_ref, sim_ref, *, bsz, tc, nsub, group, unroll):
    rows = bsz * tc
    sw = SSM_SUB * SSM_STATE
    uw = SSM_SUB * SSM_GROUP
    tiles = sw // LANE

    @pl.when(pl.program_id(1) == 0)
    def _():
        hre_ref[...] = h0re_ref[...]
        him_ref[...] = h0im_ref[...]

    u = u_ref[...].reshape(rows, nsub * uw)
    ub = u.astype(BF16)
    for s in range(nsub):
        bu = jnp.dot(ub[:, s * uw:(s + 1) * uw], wbu_ref[s], preferred_element_type=F32)
        for k in range(tiles):
            sre_ref[s * tiles + k] = bu[:, k * LANE:(k + 1) * LANE]
            sim_ref[s * tiles + k] = bu[:, sw + k * LANE:sw + (k + 1) * LANE]

    for c0 in range(0, nsub * tiles, group):
        lanes = [slice((c0 + k) * LANE, (c0 + k + 1) * LANE) for k in range(group)]
        ar = [jnp.broadcast_to(lbre_ref[:, ls], (bsz, LANE)) for ls in lanes]
        ai = [jnp.broadcast_to(lbim_ref[:, ls], (bsz, LANE)) for ls in lanes]

        def step(t, carry, c0=c0, ar=ar, ai=ai):
            rs = pl.ds(t, bsz, stride=tc)
            out = []
            for k in range(group):
                hr, hi = carry[2 * k], carry[2 * k + 1]
                nr = ar[k] * hr - ai[k] * hi + sre_ref[c0 + k, rs, :]
                ni = ar[k] * hi + ai[k] * hr + sim_ref[c0 + k, rs, :]
                sre_ref[c0 + k, rs, :] = nr
                sim_ref[c0 + k, rs, :] = ni
                out += [nr, ni]
            return tuple(out)

        init = []
        for ls in lanes:
            init += [hre_ref[:, ls], him_ref[:, ls]]
        fin = lax.fori_loop(0, tc, step, tuple(init), unroll=unroll)
        for k, ls in enumerate(lanes):
            hre_ref[:, ls] = fin[2 * k]
            him_ref[:, ls] = fin[2 * k + 1]

    d = d_ref[...]
    for s in range(nsub):
        hre = jnp.concatenate([sre_ref[s * tiles + k] for k in range(tiles)], axis=-1).astype(BF16)
        him = jnp.concatenate([sim_ref[s * tiles + k] for k in range(tiles)], axis=-1).astype(BF16)
        y = jnp.dot(hre, wcre_ref[s], preferred_element_type=F32)
        y = y + jnp.dot(him, wcim_ref[s], preferred_element_type=F32)
        y = y + d[:, s * uw:(s + 1) * uw] * u[:, s * uw:(s + 1) * uw]
        y_ref[:, :, s * uw:(s + 1) * uw] = _gelu_tanh(y).astype(BF16).reshape(bsz, tc, uw)


def s5_mixer(p3, h0_re, h0_im, wbu, lb_re, lb_im, wc_re, wc_im, d_skip, *, tc):
    bsz, t, _ = p3.shape
    nsub = 4
    uw = nsub * SSM_SUB * SSM_GROUP
    sw = nsub * SSM_SUB * SSM_STATE
    nblk = wbu.shape[0] // nsub
    bw = uw * nblk
    kern = functools.partial(_s5_kernel, bsz=bsz, tc=tc, nsub=nsub, group=4, unroll=8)
    st_spec = pl.BlockSpec((bsz, sw), lambda g, c: (0, g))
    return pl.pallas_call(
        kern,
        grid=(nblk, t // tc),
        in_specs=[pl.BlockSpec((bsz, tc, uw), lambda g, c: (0, c, AB_U // uw + g)),
                  st_spec, st_spec,
                  pl.BlockSpec((nsub,) + wbu.shape[1:], lambda g, c: (g, 0, 0)),
                  pl.BlockSpec((1, sw), lambda g, c: (0, g)),
                  pl.BlockSpec((1, sw), lambda g, c: (0, g)),
                  pl.BlockSpec((nsub,) + wc_re.shape[1:], lambda g, c: (g, 0, 0)),
                  pl.BlockSpec((nsub,) + wc_im.shape[1:], lambda g, c: (g, 0, 0)),
                  pl.BlockSpec((1, uw), lambda g, c: (0, g))],
        out_specs=[pl.BlockSpec((bsz, tc, uw), lambda g, c: (0, c, g)), st_spec, st_spec],
        out_shape=[jax.ShapeDtypeStruct((bsz, t, bw), BF16),
                   jax.ShapeDtypeStruct(h0_re.shape, F32),
                   jax.ShapeDtypeStruct(h0_im.shape, F32)],
        scratch_shapes=[pltpu.VMEM((sw // LANE, bsz * tc, LANE), F32),
                        pltpu.VMEM((sw // LANE, bsz * tc, LANE), F32)],
        compiler_params=_cparams("parallel", "arbitrary"),
    )(p3, h0_re, h0_im, wbu, lb_re, lb_im, wc_re, wc_im, d_skip)


def _glu_kernel(y_ref, wa_ref, wb_ref, g_ref, o_ref):
    y = y_ref[...]
    ga = jnp.dot(y, wa_ref[...], preferred_element_type=F32)
    gb = jnp.dot(y, wb_ref[...], preferred_element_type=F32)
    o_ref[...] = (ga * jax.nn.sigmoid(gb) * _silu(g_ref[...])).astype(o_ref.dtype)


def glu_gate(y, wa, wb, p, *, tn=512):
    m, k = y.shape
    n = wa.shape[1]
    tm = _pick(m, (1024, 512, 256, 128))
    return pl.pallas_call(
        _glu_kernel,
        grid=(m // tm, n // tn),
        in_specs=[pl.BlockSpec((tm, k), lambda i, j: (i, 0)),
                  pl.BlockSpec((k, tn), lambda i, j: (0, j)),
                  pl.BlockSpec((k, tn), lambda i, j: (0, j)),
                  pl.BlockSpec((tm, tn), lambda i, j: (i, AB_GB // tn + j))],
        out_specs=pl.BlockSpec((tm, tn), lambda i, j: (i, j)),
        out_shape=jax.ShapeDtypeStruct((m, n), BF16),
        compiler_params=_cparams("parallel", "arbitrary"),
    )(y, wa, wb, p)


C_Q = 0
C_G = 4096
C_K = 8192
C_V = 8704
KVW = SWA_KV_HEADS * SWA_HEAD_DIM


def _k_norm_kernel(k_ref, g_ref, o_ref):
    k = k_ref[...]
    g = g_ref[...]
    for h in range(SWA_KV_HEADS):
        sl = slice(h * SWA_HEAD_DIM, (h + 1) * SWA_HEAD_DIM)
        o_ref[:, sl] = _rms(k[:, sl], g)


def swa_k_norm(pc, gk):
    m = pc.shape[0]
    tm = _pick(m, (512, 256, 128))
    return pl.pallas_call(
        _k_norm_kernel,
        grid=(m // tm,),
        in_specs=[pl.BlockSpec((tm, KVW), lambda i: (i, C_K // KVW)),
                  pl.BlockSpec((1, SWA_HEAD_DIM), lambda i: (0, 0))],
        out_specs=pl.BlockSpec((tm, KVW), lambda i: (i, 0)),
        out_shape=jax.ShapeDtypeStruct((m, KVW), F32),
        compiler_params=_cparams("parallel"),
    )(pc, gk.reshape(1, -1))


def _swa_heads(q_ref, g_ref, gq_ref, sink_ref, o_ref, keys, vals, biases, invalid_first):
    gq = gq_ref[...] * (SWA_HEAD_DIM ** -0.5)
    kb = [k.astype(BF16) for k in keys]
    vb = [v.astype(BF16) for v in vals]
    for kvh in range(SWA_KV_HEADS):
        ks = slice(kvh * SWA_HEAD_DIM, (kvh + 1) * SWA_HEAD_DIM)
        for g in range(SWA_GQ):
            h = kvh * SWA_GQ + g
            hs = slice(h * SWA_HEAD_DIM, (h + 1) * SWA_HEAD_DIM)
            qh = (_rms(q_ref[:, hs], gq)).astype(BF16)
            sink = sink_ref[h]
            ss = []
            for j, kj in enumerate(kb):
                s = lax.dot_general(qh, kj[:, ks], (((1,), (1,)), ((), ())), preferred_element_type=F32)
                s = s + biases[j][h]
                if j == 0 and invalid_first is not None:
                    s = jnp.where(invalid_first, NEG, s)
                ss.append(s)
            m = jnp.full((qh.shape[0], 1), sink, F32)
            for s in ss:
                m = jnp.maximum(m, jnp.max(s, axis=-1, keepdims=True))
            den = jnp.exp(sink - m)
            o = jnp.zeros((qh.shape[0], SWA_HEAD_DIM), F32)
            for j, s in enumerate(ss):
                e = jnp.exp(s - m)
                den = den + jnp.sum(e, axis=-1, keepdims=True)
                o = o + jnp.dot(e.astype(BF16), vb[j][:, ks], preferred_element_type=F32)
            o_ref[:, hs] = (o / den * _silu(g_ref[:, hs])).astype(o_ref.dtype)


def _swa_prompt_kernel(sink_ref, q_ref, kp_ref, kc_ref, vp_ref, vc_ref, g_ref, gq_ref, bp_ref, bc_ref, o_ref):
    first = pl.program_id(1) == 0
    _swa_heads(q_ref, g_ref, gq_ref, sink_ref, o_ref, [kp_ref[...], kc_ref[...]], [vp_ref[...], vc_ref[...]],
               [bp_ref, bc_ref], first)


def swa_attn_prompt(pc, kn, gq, sinks, bias_prev, bias_cur, *, bsz, seq):
    tq = WINDOW
    nq = seq // tq
    dm = SWA_HEADS * SWA_HEAD_DIM
    prev = lambda b, i: (b * nq + jnp.maximum(i - 1, 0), 0)
    cur = lambda b, i: (b * nq + i, 0)
    const3 = lambda b, i: (0, 0, 0)
    return pl.pallas_call(
        _swa_prompt_kernel,
        grid=(bsz, nq),
        in_specs=[pl.BlockSpec(memory_space=pltpu.SMEM),
                  pl.BlockSpec((tq, dm), lambda b, i: (b * nq + i, C_Q // dm)),
                  pl.BlockSpec((tq, KVW), prev), pl.BlockSpec((tq, KVW), cur),
                  pl.BlockSpec((tq, KVW), lambda b, i: (b * nq + jnp.maximum(i - 1, 0), C_V // KVW)),
                  pl.BlockSpec((tq, KVW), lambda b, i: (b * nq + i, C_V // KVW)),
                  pl.BlockSpec((tq, dm), lambda b, i: (b * nq + i, C_G // dm)),
                  pl.BlockSpec((1, SWA_HEAD_DIM), lambda b, i: (0, 0)),
                  pl.BlockSpec(bias_prev.shape, const3), pl.BlockSpec(bias_cur.shape, const3)],
        out_specs=pl.BlockSpec((tq, dm), cur),
        out_shape=jax.ShapeDtypeStruct((bsz * seq, dm), BF16),
        compiler_params=_cparams("parallel", "arbitrary"),
    )(sinks, pc, kn, kn, pc, pc, pc, gq.reshape(1, -1), bias_prev, bias_cur)


def _swa_step_kernel(sink_ref, q_ref, kc_ref, kn_ref, vc_ref, vn_ref, g_ref, gq_ref, bc_ref, bn_ref, o_ref):
    _swa_heads(q_ref, g_ref, gq_ref, sink_ref, o_ref, [kc_ref[...], kn_ref[...]], [vc_ref[...], vn_ref[...]],
               [bc_ref, bn_ref], None)


def swa_attn_step(pc, k_cache, kn, v_cache, gq, sinks, bias_c, bias_n, *, bsz, t):
    dm = SWA_HEADS * SWA_HEAD_DIM
    const3 = lambda b: (0, 0, 0)
    row = lambda b: (b, 0)
    return pl.pallas_call(
        _swa_step_kernel,
        grid=(bsz,),
        in_specs=[pl.BlockSpec(memory_space=pltpu.SMEM),
                  pl.BlockSpec((t, dm), lambda b: (b, C_Q // dm)),
                  pl.BlockSpec((WINDOW, KVW), row), pl.BlockSpec((t, KVW), row),
                  pl.BlockSpec((WINDOW, KVW), row),
                  pl.BlockSpec((t, KVW), lambda b: (b, C_V // KVW)),
                  pl.BlockSpec((t, dm), lambda b: (b, C_G // dm)),
                  pl.BlockSpec((1, SWA_HEAD_DIM), lambda b: (0, 0)),
                  pl.BlockSpec(bias_c.shape, const3), pl.BlockSpec(bias_n.shape, const3)],
        out_specs=pl.BlockSpec((t, dm), row),
        out_shape=jax.ShapeDtypeStruct((bsz * t, dm), BF16),
        compiler_params=_cparams("parallel"),
    )(sinks, pc, k_cache, kn, v_cache, pc, pc, gq.reshape(1, -1), bias_c, bias_n)


def _rope_tables(pos, reps=1):
    half = MLA_ROPE // 2
    inv = ROPE_THETA ** (-jnp.arange(half, dtype=F32) / half)
    ang = pos.astype(F32)[:, None] * inv
    cos, sin = jnp.cos(ang), jnp.sin(ang)
    z = jnp.zeros_like(cos)
    tabs = (jnp.concatenate([cos, cos, z, z], -1), jnp.concatenate([-sin, z, z, z], -1),
            jnp.concatenate([z, sin, z, z], -1))
    return tuple(jnp.tile(t, (reps, 1)) for t in tabs)


def _rel_bucket(rel):
    nb = REL_BUCKETS // 2
    max_exact = nb // 2
    ret = jnp.where(rel > 0, nb, 0)
    n = jnp.abs(rel)
    nf = jnp.maximum(n, 1).astype(F32)
    large = max_exact + (jnp.log(nf / max_exact) / math.log(REL_MAX_DIST / max_exact)
                         * (nb - max_exact)).astype(jnp.int32)
    large = jnp.minimum(large, nb - 1)
    return ret + jnp.where(n < max_exact, n, large)


def _rel_bias(table, rel, mask):
    b = jnp.take(table, _rel_bucket(rel), axis=0).astype(F32)
    b = jnp.where(mask[:, :, None], b, NEG)
    return jnp.transpose(b, (2, 0, 1))


def _prep_ab_weights(w_in, w_uq, w_ukv, gq, gk, w_glu, w_out):
    d = w_in.shape[0]
    z = lambda n: jnp.zeros((d, n), w_in.dtype)
    w_in_r = jnp.concatenate([w_in[:, 0:768], w_in[:, 1280:1344], z(AB_CKV - AB_KR - MLA_ROPE),
                              w_in[:, 768:1280], w_in[:, 1344:]], axis=1).astype(BF16)
    uq = w_uq.reshape(w_uq.shape[0], MLA_HEADS, MLA_QK)
    uq = jnp.pad(uq, ((0, 0), (0, 0), (0, MLA_PAD - MLA_QK))).reshape(w_uq.shape[0], -1).astype(BF16)
    ukv = w_ukv.reshape(w_ukv.shape[0], MLA_HEADS, MLA_NOPE + MLA_V)
    wn = ukv[:, :, :MLA_NOPE].reshape(w_ukv.shape[0], -1).astype(BF16)
    wv = ukv[:, :, MLA_NOPE:].reshape(w_ukv.shape[0], -1).astype(BF16)
    pad_r = lambda v: jnp.pad(v[MLA_NOPE:], (0, LANE - MLA_ROPE)).reshape(1, LANE)
    half = w_glu.shape[1] // 2
    return dict(w_in=w_in_r, w_uq=uq, wn=wn, wv=wv,
                gqn=gq[:MLA_NOPE].reshape(1, LANE), gqr=pad_r(gq),
                gkn=gk[:MLA_NOPE].reshape(1, LANE), gkr=pad_r(gk),
                wa=w_glu[:, :half].astype(BF16), wb=w_glu[:, half:].astype(BF16),
                w_out=w_out.astype(BF16))


def _prep_s5(lam_re, lam_im, log_dt, b_re, b_im, c_re, c_im):
    g = lam_re.shape[0]
    lb_re, lb_im, bbt_re, bbt_im = s5_discretize(lam_re, lam_im, log_dt, b_re, b_im)
    eye = jnp.eye(SSM_SUB, dtype=F32)
    ns = g // SSM_SUB
    bb = jnp.stack([bbt_re, bbt_im]).reshape(2, ns, SSM_SUB, SSM_GROUP, SSM_STATE)
    wbu = bb[:, :, :, :, None, :] * eye[None, None, :, None, :, None]
    wbu = jnp.transpose(wbu, (1, 2, 3, 0, 4, 5)).reshape(ns, SSM_SUB * SSM_GROUP, 2 * SSM_SUB * SSM_STATE)

    def readout(c):
        c4 = c.astype(F32).reshape(ns, SSM_SUB, SSM_GROUP, SSM_STATE)
        w = c4[:, :, :, None, :] * eye[None, :, None, :, None]
        return jnp.transpose(w, (0, 3, 4, 1, 2)).reshape(ns, SSM_SUB * SSM_STATE, SSM_SUB * SSM_GROUP)

    return dict(wbu=wbu.astype(BF16), lb_re=lb_re.reshape(1, -1), lb_im=lb_im.reshape(1, -1),
                wc_re=readout(c_re).astype(BF16), wc_im=readout(-c_im).astype(BF16))


def _ab_layer(x, w, s5w, norm_g, q_lora_g, kv_lora_g, d_skip, tabs, *, bsz, seq, past):
    m = x.shape[0]
    p = norm_matmul(x, norm_g, w["w_in"], tn=768)
    q = mla_q_prep(p, q_lora_g, w["w_uq"], tabs, w["gqn"], w["gqr"])
    lat, krp, kro = mla_latent(p, kv_lora_g, tabs)
    k, v = mla_kv_expand(lat, krp, w["wn"], w["wv"], w["gkn"], w["gkr"])
    p3 = p.reshape(bsz, seq, AB_COLS)
    gdim = s5w["lb_re"].shape[1]
    if past is None:
        a_out = mla_attn_prompt(q, k, v, p, bsz=bsz, seq=seq)
        h0 = jnp.zeros((bsz, gdim), F32)
        y, h_re, h_im = s5_mixer(p3, h0, h0, s5w["wbu"], s5w["lb_re"], s5w["lb_im"], s5w["wc_re"], s5w["wc_im"],
                                 d_skip.reshape(1, -1), tc=CHUNK)
    else:
        lat_c, kr_c, h0_re, h0_im = past
        plen = lat_c.shape[1]
        kc, vc = mla_kv_expand(lat_c.reshape(bsz * plen, -1),
                               jnp.pad(kr_c.reshape(bsz * plen, -1), ((0, 0), (0, LANE - MLA_ROPE))),
                               w["wn"], w["wv"], w["gkn"], w["gkr"])
        a_out = mla_attn_step(q, kc, vc, k, v, p, bsz=bsz, t=seq, past=plen)
        y, h_re, h_im = s5_mixer(p3, h0_re.reshape(bsz, gdim), h0_im.reshape(bsz, gdim), s5w["wbu"], s5w["lb_re"],
                                 s5w["lb_im"], s5w["wc_re"], s5w["wc_im"], d_skip.reshape(1, -1), tc=seq)
    b_out = glu_gate(y.reshape(m, -1), w["wa"], w["wb"], p)
    x_new = out_proj([a_out, b_out], w["w_out"], x)
    return x_new, (lat, kro, h_re, h_im)


def _c_layer(x, w_in, w_out, norm_g, gq, gk, sinks, biases, *, bsz, seq, past):
    pc = norm_matmul(x, norm_g, w_in, tn=768)
    kn = swa_k_norm(pc, gk)
    v = pc[:, C_V:C_V + KVW]
    if past is None:
        a = swa_attn_prompt(pc, kn, gq, sinks, *biases, bsz=bsz, seq=seq)
        k_buf = kn.reshape(bsz, seq, KVW)[:, -WINDOW:]
        v_buf = v.reshape(bsz, seq, KVW)[:, -WINDOW:]
    else:
        k_c = past[0].reshape(bsz * WINDOW, KVW)
        v_c = past[1].reshape(bsz * WINDOW, KVW)
        a = swa_attn_step(pc, k_c, kn, v_c, gq, sinks, *biases, bsz=bsz, t=seq)
        k_buf = jnp.concatenate([k_c.reshape(bsz, WINDOW, KVW), kn.reshape(bsz, seq, KVW)], axis=1)[:, -WINDOW:]
        v_buf = jnp.concatenate([v_c.reshape(bsz, WINDOW, KVW), v.reshape(bsz, seq, KVW)], axis=1)[:, -WINDOW:]
    x_new = out_proj([a], w_out, x)
    shp = (bsz, WINDOW, SWA_KV_HEADS, SWA_HEAD_DIM)
    return x_new, (k_buf.reshape(shp), v_buf.reshape(shp))


def kernel(x_prompt, x_sample, cache_mla_latent, cache_mla_krope, state_ssm_re, state_ssm_im, cache_swa_k, cache_swa_v, rel_bias_table, ab_norm, ab_w_in, ab_q_lora_norm, ab_kv_lora_norm, ab_w_uq, ab_w_ukv, ab_q_norm, ab_k_norm, ssm_lambda_re, ssm_lambda_im, ssm_b_re, ssm_b_im, ssm_c_re, ssm_c_im, ssm_log_dt, ssm_d, ssm_w_glu, ab_w_out, c_norm, c_w_in, c_q_norm, c_k_norm, c_sinks, c_w_out):
    bp, lp, d = x_prompt.shape
    bs, ts, _ = x_sample.shape
    past_len = cache_mla_latent.shape[2]
    depth = ab_norm.shape[0] + c_norm.shape[0]
    assert lp % 256 == 0 and (bs * ts) % 128 == 0 and past_len % 512 == 0 and d == SWA_HEADS * SWA_HEAD_DIM

    q_pos = past_len + np.arange(ts)
    assert np.all((np.arange(past_len + ts)[None, :] // CHUNK) <= (q_pos[:, None] // CHUNK))
    k_pos = past_len - WINDOW + np.arange(WINDOW + ts)
    qch, kch = q_pos[:, None] // CHUNK, k_pos[None, :] // CHUNK
    assert np.all((kch <= qch) & (kch >= qch - WINDOW // CHUNK))

    tabs_p = _rope_tables(jnp.arange(lp))
    tabs_s = _rope_tables(past_len + jnp.arange(ts), reps=bs)

    qi = np.arange(WINDOW)[:, None]
    kj = np.arange(2 * WINDOW)[None, :]
    band = (kj // CHUNK - 2 <= qi // CHUNK) & (kj // CHUNK >= qi // CHUNK)
    bias_p = _rel_bias(rel_bias_table, jnp.asarray(kj - WINDOW - qi), jnp.asarray(band))
    biases_p = (bias_p[:, :, :WINDOW], bias_p[:, :, WINDOW:])
    bias_s = _rel_bias(rel_bias_table, jnp.asarray(k_pos[None, :] - q_pos[:, None]),
                       jnp.ones((ts, WINDOW + ts), bool))
    biases_s = (bias_s[:, :, :WINDOW], bias_s[:, :, WINDOW:])

    xp = x_prompt.reshape(bp * lp, d)
    xs = x_sample.reshape(bs * ts, d)
    outs_p = [[] for _ in range(6)]
    outs_s = [[] for _ in range(6)]
    for layer in range(depth):
        i = layer // 2
        if layer % 2 == 0:
            w = _prep_ab_weights(ab_w_in[i], ab_w_uq[i], ab_w_ukv[i], ab_q_norm[i], ab_k_norm[i],
                                 ssm_w_glu[i], ab_w_out[i])
            s5w = _prep_s5(ssm_lambda_re[i], ssm_lambda_im[i], ssm_log_dt[i], ssm_b_re[i], ssm_b_im[i],
                           ssm_c_re[i], ssm_c_im[i])
            args = (w, s5w, ab_norm[i], ab_q_lora_norm[i], ab_kv_lora_norm[i], ssm_d[i])
            xp, (a, b, c, e) = _ab_layer(xp, *args, tabs_p, bsz=bp, seq=lp, past=None)
            outs_p[0].append(a.reshape(bp, lp, -1)); outs_p[1].append(b.reshape(bp, lp, -1))
            outs_p[2].append(c.reshape(bp, -1, SSM_STATE)); outs_p[3].append(e.reshape(bp, -1, SSM_STATE))
            past = (cache_mla_latent[i], cache_mla_krope[i], state_ssm_re[i], state_ssm_im[i])
            xs, (a, b, c, e) = _ab_layer(xs, *args, tabs_s, bsz=bs, seq=ts, past=past)
            outs_s[0].append(a.reshape(bs, ts, -1)); outs_s[1].append(b.reshape(bs, ts, -1))
            outs_s[2].append(c.reshape(bs, -1, SSM_STATE)); outs_s[3].append(e.reshape(bs, -1, SSM_STATE))
        else:
            cw = c_w_in[i]
            w_in = jnp.concatenate([cw[:, :d], cw[:, d + 2 * KVW:], cw[:, d:d + 2 * KVW]], axis=1).astype(BF16)
            w_out = c_w_out[i].astype(BF16)
            args = (w_in, w_out, c_norm[i], c_q_norm[i], c_k_norm[i], c_sinks[i])
            xp, (a, b) = _c_layer(xp, *args, biases_p, bsz=bp, seq=lp, past=None)
            outs_p[4].append(a); outs_p[5].append(b)
            xs, (a, b) = _c_layer(xs, *args, biases_s, bsz=bs, seq=ts, past=(cache_swa_k[i], cache_swa_v[i]))
            outs_s[4].append(a); outs_s[5].append(b)
    return (xp.reshape(bp, lp, d), xs.reshape(bs, ts, d),
            *[jnp.stack(o) for o in outs_p], *[jnp.stack(o) for o in outs_s])
```

```python
import functools
import math

import jax
import jax.numpy as jnp
import numpy as np
from jax import lax
from jax.experimental import pallas as pl
from jax.experimental.pallas import tpu as pltpu

F32 = jnp.float32
BF16 = jnp.bfloat16

EPS = 1e-6
CHUNK = 64
ROPE_THETA = 10000.0
MLA_HEADS = 16
MLA_NOPE = 128
MLA_ROPE = 64
MLA_QK = MLA_NOPE + MLA_ROPE
MLA_V = 128
MLA_PAD = 256
SSM_GROUP = 16
SSM_STATE = 64
SSM_SUB = 8
SWA_HEADS = 64
SWA_KV_HEADS = 8
SWA_GQ = SWA_HEADS // SWA_KV_HEADS
SWA_HEAD_DIM = 64
WINDOW = 128
REL_BUCKETS = 32
REL_MAX_DIST = 128
NEG = -1e30
LANE = 128
VMEM_LIMIT = 56 * 1024 * 1024

AB_CQ = 0
AB_KR = 768
AB_CKV = 1024
AB_GA = 1536
AB_U = 3584
AB_GB = 5632
AB_COLS = 7680


def _cparams(*sem):
    return pltpu.CompilerParams(dimension_semantics=sem, vmem_limit_bytes=VMEM_LIMIT)


def _rms(x, g):
    return x * lax.rsqrt(jnp.mean(x * x, axis=-1, keepdims=True) + EPS) * g


def _silu(x):
    return x * jax.nn.sigmoid(x)


def _gelu_tanh(x):
    return 0.5 * x * (1.0 + jnp.tanh(math.sqrt(2.0 / math.pi) * (x + 0.044715 * (x * x * x))))


def _pick(n, prefs):
    for p in prefs:
        if n % p == 0:
            return p
    raise ValueError(f"no tile for {n} in {prefs}")


def _norm_mm_kernel(x_ref, g_ref, w_ref, o_ref, h_ref):
    @pl.when(pl.program_id(1) == 0)
    def _():
        h_ref[...] = _rms(x_ref[...], g_ref[...]).astype(BF16)

    o_ref[...] = jnp.dot(h_ref[...], w_ref[...], preferred_element_type=F32).astype(o_ref.dtype)


def norm_matmul(x, g, w, *, tn, out_dtype=F32):
    m, k = x.shape
    n = w.shape[1]
    tm = _pick(m, (512, 256, 128))
    return pl.pallas_call(
        _norm_mm_kernel, name="norm_mm",
        grid=(m // tm, n // tn),
        in_specs=[pl.BlockSpec((tm, k), lambda i, j: (i, 0)),
                  pl.BlockSpec((1, k), lambda i, j: (0, 0)),
                  pl.BlockSpec((k, tn), lambda i, j: (0, j))],
        out_specs=pl.BlockSpec((tm, tn), lambda i, j: (i, j)),
        out_shape=jax.ShapeDtypeStruct((m, n), out_dtype),
        scratch_shapes=[pltpu.VMEM((tm, k), BF16)],
        compiler_params=_cparams("parallel", "arbitrary"),
    )(x, g.reshape(1, k), w)


def _out_proj_kernel(*refs, n_lhs):
    lhs = refs[:n_lhs]
    w_ref, x_ref, o_ref = refs[n_lhs:]
    acc = x_ref[...]
    off = 0
    for a in lhs:
        kk = a.shape[1]
        acc = acc + jnp.dot(a[...], w_ref[off:off + kk, :], preferred_element_type=F32)
        off += kk
    o_ref[...] = acc


def out_proj(lhs, w, x, *, tn=512):
    m, n = x.shape
    k = w.shape[0]
    tm = _pick(m, (1024, 512, 256, 128))
    in_specs = [pl.BlockSpec((tm, a.shape[1]), lambda i, j: (i, 0)) for a in lhs]
    in_specs += [pl.BlockSpec((k, tn), lambda i, j: (0, j)),
                 pl.BlockSpec((tm, tn), lambda i, j: (i, j))]
    return pl.pallas_call(
        functools.partial(_out_proj_kernel, n_lhs=len(lhs)), name="out_proj",
        grid=(m // tm, n // tn),
        in_specs=in_specs,
        out_specs=pl.BlockSpec((tm, tn), lambda i, j: (i, j)),
        out_shape=jax.ShapeDtypeStruct((m, n), F32),
        compiler_params=_cparams("parallel", "arbitrary"),
    )(*lhs, w, x)


def _rope_slab(r, c, s1, s2):
    return r * c + pltpu.roll(r, 96, 1) * s1 + pltpu.roll(r, 32, 1) * s2


def _q_prep_kernel(cq_ref, g_ref, w_ref, c_ref, s1_ref, s2_ref, gn_ref, gr_ref, q_ref):
    h = _rms(cq_ref[...], g_ref[...]).astype(BF16)
    c, s1, s2 = c_ref[...], s1_ref[...], s2_ref[...]
    scale = MLA_QK ** -0.5 * math.log2(math.e)
    gn = gn_ref[...] * scale
    gr = gr_ref[...] * scale
    for hd in range(MLA_HEADS):
        lo = hd * MLA_PAD
        nope = jnp.dot(h, w_ref[:, lo:lo + LANE], preferred_element_type=F32)
        rr = _rope_slab(jnp.dot(h, w_ref[:, lo + LANE:lo + MLA_PAD], preferred_element_type=F32), c, s1, s2)
        ss = jnp.sum(nope * nope, axis=-1, keepdims=True) + jnp.sum(rr * rr, axis=-1, keepdims=True)
        rs = lax.rsqrt(ss * (1.0 / MLA_QK) + EPS)
        q_ref[:, lo:lo + LANE] = (nope * rs * gn).astype(BF16)
        q_ref[:, lo + LANE:lo + MLA_PAD] = (rr * rs * gr).astype(BF16)


def mla_q_prep(p, g, w_uq, tabs, gn, gr):
    m = p.shape[0]
    period = tabs[0].shape[0]
    tm = _pick(period, (256, 128))
    nb = period // tm
    tab_spec = pl.BlockSpec((tm, LANE), lambda i: (i % nb, 0))
    vec = lambda n: pl.BlockSpec((1, n), lambda i: (0, 0))
    return pl.pallas_call(
        _q_prep_kernel, name="mla_q_prep",
        grid=(m // tm,),
        in_specs=[pl.BlockSpec((tm, 768), lambda i: (i, 0)), vec(768),
                  pl.BlockSpec(w_uq.shape, lambda i: (0, 0)),
                  tab_spec, tab_spec, tab_spec, vec(LANE), vec(LANE)],
        out_specs=pl.BlockSpec((tm, MLA_HEADS * MLA_PAD), lambda i: (i, 0)),
        out_shape=jax.ShapeDtypeStruct((m, MLA_HEADS * MLA_PAD), BF16),
        compiler_params=_cparams("parallel"),
    )(p, g.reshape(1, -1), w_uq, *tabs, gn, gr)


def _latent_kernel(ckv_ref, kr_ref, g_ref, c_ref, s1_ref, s2_ref, lat_ref, krp_ref, kro_ref):
    lat_ref[...] = _rms(ckv_ref[...], g_ref[...])
    rr = _rope_slab(kr_ref[...], c_ref[...], s1_ref[...], s2_ref[...])
    krp_ref[...] = rr
    kro_ref[...] = rr[:, :MLA_ROPE]


def mla_latent(p, g, tabs):
    m = p.shape[0]
    period = tabs[0].shape[0]
    tm = _pick(period, (512, 256, 128))
    nb = period // tm
    tab_spec = pl.BlockSpec((tm, LANE), lambda i: (i % nb, 0))
    return pl.pallas_call(
        _latent_kernel, name="mla_latent",
        grid=(m // tm,),
        in_specs=[pl.BlockSpec((tm, 512), lambda i: (i, AB_CKV // 512)),
                  pl.BlockSpec((tm, LANE), lambda i: (i, AB_KR // LANE)),
                  pl.BlockSpec((1, 512), lambda i: (0, 0)),
                  tab_spec, tab_spec, tab_spec],
        out_specs=[pl.BlockSpec((tm, 512), lambda i: (i, 0)),
                   pl.BlockSpec((tm, LANE), lambda i: (i, 0)),
                   pl.BlockSpec((tm, MLA_ROPE), lambda i: (i, 0))],
        out_shape=[jax.ShapeDtypeStruct((m, 512), F32),
                   jax.ShapeDtypeStruct((m, LANE), F32),
                   jax.ShapeDtypeStruct((m, MLA_ROPE), F32)],
        compiler_params=_cparams("parallel"),
    )(p, p, g.reshape(1, -1), *tabs)


def _kv_expand_kernel(lat_ref, krp_ref, wn_ref, wv_ref, gn_ref, gr_ref, k_ref, v_ref):
    lat = lat_ref[...].astype(BF16)
    kr = krp_ref[...]
    ss_r = jnp.sum(kr * kr, axis=-1, keepdims=True)
    gn, gr = gn_ref[...], gr_ref[...]
    for hd in range(MLA_HEADS):
        kn = jnp.dot(lat, wn_ref[:, hd * LANE:(hd + 1) * LANE], preferred_element_type=F32)
        rs = lax.rsqrt((jnp.sum(kn * kn, axis=-1, keepdims=True) + ss_r) * (1.0 / MLA_QK) + EPS)
        lo = hd * MLA_PAD
        k_ref[:, lo:lo + LANE] = (kn * rs * gn).astype(BF16)
        k_ref[:, lo + LANE:lo + MLA_PAD] = (kr * rs * gr).astype(BF16)
        v_ref[:, hd * LANE:(hd + 1) * LANE] = jnp.dot(
            lat, wv_ref[:, hd * LANE:(hd + 1) * LANE], preferred_element_type=F32).astype(BF16)


def mla_kv_expand(lat, krp, wn, wv, gn, gr):
    m = lat.shape[0]
    tm = _pick(m, (512, 256, 128))
    vec = pl.BlockSpec((1, LANE), lambda i: (0, 0))
    return pl.pallas_call(
        _kv_expand_kernel, name="mla_kv_expand",
        grid=(m // tm,),
        in_specs=[pl.BlockSpec((tm, 512), lambda i: (i, 0)),
                  pl.BlockSpec((tm, LANE), lambda i: (i, 0)),
                  pl.BlockSpec(wn.shape, lambda i: (0, 0)),
                  pl.BlockSpec(wv.shape, lambda i: (0, 0)), vec, vec],
        out_specs=[pl.BlockSpec((tm, MLA_HEADS * MLA_PAD), lambda i: (i, 0)),
                   pl.BlockSpec((tm, MLA_HEADS * MLA_V), lambda i: (i, 0))],
        out_shape=[jax.ShapeDtypeStruct((m, MLA_HEADS * MLA_PAD), BF16),
                   jax.ShapeDtypeStruct((m, MLA_HEADS * MLA_V), BF16)],
        compiler_params=_cparams("parallel"),
    )(lat, krp, wn, wv, gn, gr)


def _softmax_step(q, kblk, vblk, carry, mask=None):
    m, l, acc = carry
    s = lax.dot_general(q, kblk, (((1,), (1,)), ((), ())), preferred_element_type=F32)
    if mask is not None:
        s = jnp.where(mask, s, NEG)
    m_new = jnp.maximum(m, jnp.max(s, axis=-1, keepdims=True))
    p = jnp.exp2(s - m_new)
    alpha = jnp.exp2(m - m_new)
    l = alpha * l + jnp.sum(p, axis=-1, keepdims=True)
    acc = alpha * acc + jnp.dot(p.astype(BF16), vblk, preferred_element_type=F32)
    return m_new, l, acc


def _mla_attn_kernel(q_ref, k_ref, v_ref, g_ref, o_ref, *, tq):
    i = pl.program_id(2)
    q = q_ref[...]

    def body(j, carry):
        off = pl.multiple_of(j * tq, tq)
        return _softmax_step(q, k_ref[pl.ds(off, tq), :], v_ref[pl.ds(off, tq), :], carry)

    init = (jnp.full((tq, 1), NEG, F32), jnp.zeros((tq, 1), F32), jnp.zeros((tq, MLA_V), F32))
    carry = lax.fori_loop(0, i, body, init)
    off = pl.multiple_of(i * tq, tq)
    row = lax.broadcasted_iota(jnp.int32, (tq, tq), 0) // CHUNK
    col = lax.broadcasted_iota(jnp.int32, (tq, tq), 1) // CHUNK
    _, l, acc = _softmax_step(q, k_ref[pl.ds(off, tq), :], v_ref[pl.ds(off, tq), :], carry, col <= row)
    o_ref[...] = (acc / l * _silu(g_ref[...])).astype(o_ref.dtype)


def mla_attn_prompt(q, k, v, p, *, bsz, seq):
    tq = _pick(seq, (512, 256))
    nq = seq // tq
    return pl.pallas_call(
        functools.partial(_mla_attn_kernel, tq=tq), name="mla_attn",
        grid=(bsz, MLA_HEADS, nq),
        in_specs=[pl.BlockSpec((tq, MLA_PAD), lambda b, h, i: (b * nq + i, h)),
                  pl.BlockSpec((seq, MLA_PAD), lambda b, h, i: (b, h)),
                  pl.BlockSpec((seq, MLA_V), lambda b, h, i: (b, h)),
                  pl.BlockSpec((tq, MLA_V), lambda b, h, i: (b * nq + i, AB_GA // MLA_V + h))],
        out_specs=pl.BlockSpec((tq, MLA_V), lambda b, h, i: (b * nq + i, h)),
        out_shape=jax.ShapeDtypeStruct((bsz * seq, MLA_HEADS * MLA_V), BF16),
        compiler_params=_cparams("parallel", "parallel", "arbitrary"),
    )(q, k, v, p)


def _mla_attn_step_kernel(q_ref, kc_ref, vc_ref, kn_ref, vn_ref, g_ref, o_ref):
    q = q_ref[...]
    t = q.shape[0]
    init = (jnp.full((t, 1), NEG, F32), jnp.zeros((t, 1), F32), jnp.zeros((t, MLA_V), F32))
    carry = _softmax_step(q, kc_ref[...], vc_ref[...], init)
    _, l, acc = _softmax_step(q, kn_ref[...], vn_ref[...], carry)
    o_ref[...] = (acc / l * _silu(g_ref[...])).astype(o_ref.dtype)


def mla_attn_step(q, kc, vc, kn, vn, p, *, bsz, t, past):
    return pl.pallas_call(
        _mla_attn_step_kernel, name="mla_attn_step",
        grid=(bsz, MLA_HEADS),
        in_specs=[pl.BlockSpec((t, MLA_PAD), lambda b, h: (b, h)),
                  pl.BlockSpec((past, MLA_PAD), lambda b, h: (b, h)),
                  pl.BlockSpec((past, MLA_V), lambda b, h: (b, h)),
                  pl.BlockSpec((t, MLA_PAD), lambda b, h: (b, h)),
                  pl.BlockSpec((t, MLA_V), lambda b, h: (b, h)),
                  pl.BlockSpec((t, MLA_V), lambda b, h: (b, AB_GA // MLA_V + h))],
        out_specs=pl.BlockSpec((t, MLA_V), lambda b, h: (b, h)),
        out_shape=jax.ShapeDtypeStruct((bsz * t, MLA_HEADS * MLA_V), BF16),
        compiler_params=_cparams("parallel", "parallel"),
    )(q, kc, vc, kn, vn, p)


def _s5_disc_kernel(lre_ref, lim_ref, ldt_ref, bre_ref, bim_ref, lbre_ref, lbim_ref, bbre_ref, bbim_ref):
    lam_re = jnp.minimum(lre_ref[...], -1e-4)
    lam_im = lim_ref[...]
    dt = jnp.exp(ldt_ref[...])
    mag = jnp.exp(lam_re * dt)
    lb_re = mag * jnp.cos(lam_im * dt)
    lb_im = mag * jnp.sin(lam_im * dt)
    nr, ni = lb_re - 1.0, lb_im
    den = lam_re * lam_re + lam_im * lam_im
    f_re = (nr * lam_re + ni * lam_im) / den
    f_im = (ni * lam_re - nr * lam_im) / den
    b_re, b_im = bre_ref[...], bim_ref[...]
    lbre_ref[...] = lb_re
    lbim_ref[...] = lb_im
    bbre_ref[...] = f_re * b_re - f_im * b_im
    bbim_ref[...] = f_re * b_im + f_im * b_re


def s5_discretize(lam_re, lam_im, log_dt, b_re, b_im):
    g, pdim = lam_re.shape
    bt_re = jnp.swapaxes(b_re, 1, 2)
    bt_im = jnp.swapaxes(b_im, 1, 2)
    s3 = jax.ShapeDtypeStruct((g, 1, pdim), F32)
    sb = jax.ShapeDtypeStruct(bt_re.shape, F32)
    return pl.pallas_call(
        _s5_disc_kernel, name="s5_disc", out_shape=[s3, s3, sb, sb],
    )(lam_re.reshape(g, 1, pdim), lam_im.reshape(g, 1, pdim), log_dt.reshape(g, 1, 1), bt_re, bt_im)


def _s5_kernel(u_ref, h0re_ref, h0im_ref, wbu_ref, lbre_ref, lbim_ref, wcre_ref, wcim_ref, d_ref,
               y_ref, hre_ref, him_ref, sre_ref, sim_ref, *, bsz, tc, nsub, group, unroll):
    rows = bsz * tc
    sw = SSM_SUB * SSM_STATE
    uw = SSM_SUB * SSM_GROUP
    tiles = sw // LANE

    @pl.when(pl.program_id(1) == 0)
    def _():
        hre_ref[...] = h0re_ref[...]
        him_ref[...] = h0im_ref[...]

    u = u_ref[...].reshape(rows, nsub * uw)
    ub = u.astype(BF16)
    for s in range(nsub):
        bu = jnp.dot(ub[:, s * uw:(s + 1) * uw], wbu_ref[s], preferred_element_type=F32)
        for k in range(tiles):
            sre_ref[s * tiles + k] = bu[:, k * LANE:(k + 1) * LANE]
            sim_ref[s * tiles + k] = bu[:, sw + k * LANE:sw + (k + 1) * LANE]

    for c0 in range(0, nsub * tiles, group):
        lanes = [slice((c0 + k) * LANE, (c0 + k + 1) * LANE) for k in range(group)]
        ar = [jnp.broadcast_to(lbre_ref[:, ls], (bsz, LANE)) for ls in lanes]
        ai = [jnp.broadcast_to(lbim_ref[:, ls], (bsz, LANE)) for ls in lanes]

        def step(t, carry, c0=c0, ar=ar, ai=ai):
            rs = pl.ds(t, bsz, stride=tc)
            out = []
            for k in range(group):
                hr, hi = carry[2 * k], carry[2 * k + 1]
                nr = ar[k] * hr - ai[k] * hi + sre_ref[c0 + k, rs, :]
                ni = ar[k] * hi + ai[k] * hr + sim_ref[c0 + k, rs, :]
                sre_ref[c0 + k, rs, :] = nr
                sim_ref[c0 + k, rs, :] = ni
                out += [nr, ni]
            return tuple(out)

        init = []
        for ls in lanes:
            init += [hre_ref[:, ls], him_ref[:, ls]]
        fin = lax.fori_loop(0, tc, step, tuple(init), unroll=unroll)
        for k, ls in enumerate(lanes):
            hre_ref[:, ls] = fin[2 * k]
            him_ref[:, ls] = fin[2 * k + 1]

    d = d_ref[...]
    for s in range(nsub):
        hre = jnp.concatenate([sre_ref[s * tiles + k] for k in range(tiles)], axis=-1).astype(BF16)
        him = jnp.concatenate([sim_ref[s * tiles + k] for k in range(tiles)], axis=-1).astype(BF16)
        y = jnp.dot(hre, wcre_ref[s], preferred_element_type=F32)
        y = y + jnp.dot(him, wcim_ref[s], preferred_element_type=F32)
        y = y + d[:, s * uw:(s + 1) * uw] * u[:, s * uw:(s + 1) * uw]
        y_ref[:, :, s * uw:(s + 1) * uw] = _gelu_tanh(y).astype(BF16).reshape(bsz, tc, uw)


def s5_mixer(p3, h0_re, h0_im, wbu, lb_re, lb_im, wc_re, wc_im, d_skip, *, tc):
    bsz, t, _ = p3.shape
    nsub = 4
    uw = nsub * SSM_SUB * SSM_GROUP
    sw = nsub * SSM_SUB * SSM_STATE
    nblk = wbu.shape[0] // nsub
    bw = uw * nblk
    kern = functools.partial(_s5_kernel, bsz=bsz, tc=tc, nsub=nsub, group=8, unroll=4)
    st_spec = pl.BlockSpec((bsz, sw), lambda g, c: (0, g))
    return pl.pallas_call(
        kern, name="s5_mixer",
        grid=(nblk, t // tc),
        in_specs=[pl.BlockSpec((bsz, tc, uw), lambda g, c: (0, c, AB_U // uw + g)),
                  st_spec, st_spec,
                  pl.BlockSpec((nsub,) + wbu.shape[1:], lambda g, c: (g, 0, 0)),
                  pl.BlockSpec((1, sw), lambda g, c: (0, g)),
                  pl.BlockSpec((1, sw), lambda g, c: (0, g)),
                  pl.BlockSpec((nsub,) + wc_re.shape[1:], lambda g, c: (g, 0, 0)),
                  pl.BlockSpec((nsub,) + wc_im.shape[1:], lambda g, c: (g, 0, 0)),
                  pl.BlockSpec((1, uw), lambda g, c: (0, g))],
        out_specs=[pl.BlockSpec((bsz, tc, uw), lambda g, c: (0, c, g)), st_spec, st_spec],
        out_shape=[jax.ShapeDtypeStruct((bsz, t, bw), BF16),
                   jax.ShapeDtypeStruct(h0_re.shape, F32),
                   jax.ShapeDtypeStruct(h0_im.shape, F32)],
        scratch_shapes=[pltpu.VMEM((sw // LANE, bsz * tc, LANE), F32),
                        pltpu.VMEM((sw // LANE, bsz * tc, LANE), F32)],
        compiler_params=_cparams("parallel", "arbitrary"),
    )(p3, h0_re, h0_im, wbu, lb_re, lb_im, wc_re, wc_im, d_skip)


def _glu_kernel(y_ref, wa_ref, wb_ref, g_ref, o_ref):
    y = y_ref[...]
    ga = jnp.dot(y, wa_ref[...], preferred_element_type=F32)
    gb = jnp.dot(y, wb_ref[...], preferred_element_type=F32)
    o_ref[...] = (ga * jax.nn.sigmoid(gb) * _silu(g_ref[...])).astype(o_ref.dtype)


def glu_gate(y, wa, wb, p, *, tn=512):
    m, k = y.shape
    n = wa.shape[1]
    tm = _pick(m, (1024, 512, 256, 128))
    return pl.pallas_call(
        _glu_kernel, name="glu_gate",
        grid=(m // tm, n // tn),
        in_specs=[pl.BlockSpec((tm, k), lambda i, j: (i, 0)),
                  pl.BlockSpec((k, tn), lambda i, j: (0, j)),
                  pl.BlockSpec((k, tn), lambda i, j: (0, j)),
                  pl.BlockSpec((tm, tn), lambda i, j: (i, AB_GB // tn + j))],
        out_specs=pl.BlockSpec((tm, tn), lambda i, j: (i, j)),
        out_shape=jax.ShapeDtypeStruct((m, n), BF16),
        compiler_params=_cparams("parallel", "arbitrary"),
    )(y, wa, wb, p)


C_Q = 0
C_G = 4096
C_K = 8192
C_V = 8704
KVW = SWA_KV_HEADS * SWA_HEAD_DIM


def _k_norm_kernel(k_ref, g_ref, o_ref):
    k = k_ref[...]
    g = g_ref[...]
    for h in range(SWA_KV_HEADS):
        sl = slice(h * SWA_HEAD_DIM, (h + 1) * SWA_HEAD_DIM)
        o_ref[:, sl] = _rms(k[:, sl], g)


def swa_k_norm(pc, gk):
    m = pc.shape[0]
    tm = _pick(m, (512, 256, 128))
    return pl.pallas_call(
        _k_norm_kernel, name="swa_k_norm",
        grid=(m // tm,),
        in_specs=[pl.BlockSpec((tm, KVW), lambda i: (i, C_K // KVW)),
                  pl.BlockSpec((1, SWA_HEAD_DIM), lambda i: (0, 0))],
        out_specs=pl.BlockSpec((tm, KVW), lambda i: (i, 0)),
        out_shape=jax.ShapeDtypeStruct((m, KVW), F32),
        compiler_params=_cparams("parallel"),
    )(pc, gk.reshape(1, -1))


def _swa_heads(q_ref, g_ref, gq_ref, sink_ref, o_ref, keys, vals, biases, invalid_first):
    gq = gq_ref[...] * (SWA_HEAD_DIM ** -0.5)
    kb = [k.astype(BF16) for k in keys]
    vb = [v.astype(BF16) for v in vals]
    for kvh in range(SWA_KV_HEADS):
        ks = slice(kvh * SWA_HEAD_DIM, (kvh + 1) * SWA_HEAD_DIM)
        for g in range(SWA_GQ):
            h = kvh * SWA_GQ + g
            hs = slice(h * SWA_HEAD_DIM, (h + 1) * SWA_HEAD_DIM)
            qh = (_rms(q_ref[:, hs], gq)).astype(BF16)
            sink = sink_ref[h]
            ss = []
            for j, kj in enumerate(kb):
                s = lax.dot_general(qh, kj[:, ks], (((1,), (1,)), ((), ())), preferred_element_type=F32)
                s = s + biases[j][h]
                if j == 0 and invalid_first is not None:
                    s = jnp.where(invalid_first, NEG, s)
                ss.append(s)
            m = jnp.full((qh.shape[0], 1), sink, F32)
            for s in ss:
                m = jnp.maximum(m, jnp.max(s, axis=-1, keepdims=True))
            den = jnp.exp(sink - m)
            o = jnp.zeros((qh.shape[0], SWA_HEAD_DIM), F32)
            for j, s in enumerate(ss):
                e = jnp.exp(s - m)
                den = den + jnp.sum(e, axis=-1, keepdims=True)
                o = o + jnp.dot(e.astype(BF16), vb[j][:, ks], preferred_element_type=F32)
            o_ref[:, hs] = (o / den * _silu(g_ref[:, hs])).astype(o_ref.dtype)


def _swa_prompt_kernel(sink_ref, q_ref, kp_ref, kc_ref, vp_ref, vc_ref, g_ref, gq_ref, b_ref, o_ref, *, pair_batch):
    tq = q_ref.shape[0]
    kcat = jnp.concatenate([kp_ref[...], kc_ref[...]], axis=0)
    vcat = jnp.concatenate([vp_ref[...], vc_ref[...]], axis=0)
    seg_r = lax.broadcasted_iota(jnp.int32, (LANE, LANE), 0) // SWA_HEAD_DIM
    seg_c = lax.broadcasted_iota(jnp.int32, (LANE, LANE), 1) // SWA_HEAD_DIM
    seg_ones = (seg_r == seg_c).astype(BF16)
    left = lax.broadcasted_iota(jnp.int32, (1, LANE), 1) < SWA_HEAD_DIM
    gq = gq_ref[...] * (SWA_HEAD_DIM ** -0.5)
    nt = (((1,), (1,)), ((), ()))

    def halves(slab, half):
        own = jnp.where(left if half == 0 else jnp.logical_not(left), slab, 0.0)
        other = pltpu.roll(own, SWA_HEAD_DIM, 1)
        lo, hi = (own, other) if half == 0 else (other, own)
        return lo.astype(BF16), hi.astype(BF16)

    kv = {}

    def kv_halves(kvh):
        if kvh not in kv:
            ks = slice((kvh // 2) * LANE, (kvh // 2 + 1) * LANE)
            kv.clear()
            kv[kvh] = halves(kcat[:, ks], kvh % 2) + halves(vcat[:, ks], kvh % 2)
        return kv[kvh]

    def scores(pair):
        sl = slice(pair * LANE, (pair + 1) * LANE)
        k_lo, k_hi, _, _ = kv_halves(pair // (SWA_GQ // 2))
        q = q_ref[:, sl]
        x2 = q * q
        x2_hi = x2.astype(BF16)
        x2_lo = (x2 - x2_hi.astype(F32)).astype(BF16)
        ss = (jnp.dot(x2_hi, seg_ones, preferred_element_type=F32)
              + jnp.dot(x2_lo, seg_ones, preferred_element_type=F32))
        qn = (q * lax.rsqrt(ss * (1.0 / SWA_HEAD_DIM) + EPS) * gq).astype(BF16)
        return [lax.dot_general(qn, k_lo, nt, preferred_element_type=F32) + b_ref[2 * pair],
                lax.dot_general(qn, k_hi, nt, preferred_element_type=F32) + b_ref[2 * pair + 1]]

    def finish(pairs, ss, vhs):
        sinks = [sink_ref[2 * p + j] for p in pairs for j in range(2)]
        flat = [s for pr in ss for s in pr]
        ms = [jnp.maximum(jnp.max(s, axis=-1, keepdims=True), sk) for s, sk in zip(flat, sinks)]
        es = [jnp.exp(s - m) for s, m in zip(flat, ms)]
        ds = [jnp.sum(e, axis=-1, keepdims=True) + jnp.exp(sk - m) for e, m, sk in zip(es, ms, sinks)]
        for i, p in enumerate(pairs):
            v_lo, v_hi = vhs[i]
            o = (jnp.dot(es[2 * i].astype(BF16), v_lo, preferred_element_type=F32)
                 + jnp.dot(es[2 * i + 1].astype(BF16), v_hi, preferred_element_type=F32))
            inv = jnp.where(left, 1.0 / ds[2 * i], 1.0 / ds[2 * i + 1])
            sl = slice(p * LANE, (p + 1) * LANE)
            o_ref[:, sl] = (o * inv * _silu(g_ref[:, sl])).astype(o_ref.dtype)

    npairs = SWA_HEADS // 2
    batches = [list(range(b, b + pair_batch)) for b in range(0, npairs, pair_batch)]
    nxt = ([scores(p) for p in batches[0]], [kv_halves(p // (SWA_GQ // 2))[2:] for p in batches[0]])
    for bi, pairs in enumerate(batches):
        cur = nxt
        if bi + 1 < len(batches):
            nb = batches[bi + 1]
            nxt = ([scores(p) for p in nb], [kv_halves(p // (SWA_GQ // 2))[2:] for p in nb])
        finish(pairs, *cur)


def swa_attn_prompt(pc, kn, gq, sinks, bias, *, bsz, seq):
    tq = WINDOW
    nq = seq // tq
    dm = SWA_HEADS * SWA_HEAD_DIM
    prev = lambda b, i: (b * nq + jnp.maximum(i - 1, 0), 0)
    cur = lambda b, i: (b * nq + i, 0)
    return pl.pallas_call(
        functools.partial(_swa_prompt_kernel, pair_batch=2), name="swa_attn",
        grid=(bsz, nq),
        in_specs=[pl.BlockSpec(memory_space=pltpu.SMEM),
                  pl.BlockSpec((tq, dm), lambda b, i: (b * nq + i, C_Q // dm)),
                  pl.BlockSpec((tq, KVW), prev), pl.BlockSpec((tq, KVW), cur),
                  pl.BlockSpec((tq, KVW), lambda b, i: (b * nq + jnp.maximum(i - 1, 0), C_V // KVW)),
                  pl.BlockSpec((tq, KVW), lambda b, i: (b * nq + i, C_V // KVW)),
                  pl.BlockSpec((tq, dm), lambda b, i: (b * nq + i, C_G // dm)),
                  pl.BlockSpec((1, LANE), lambda b, i: (0, 0)),
                  pl.BlockSpec((None,) + bias.shape[1:], lambda b, i: (jnp.minimum(i, 1), 0, 0, 0))],
        out_specs=pl.BlockSpec((tq, dm), cur),
        out_shape=jax.ShapeDtypeStruct((bsz * seq, dm), BF16),
        compiler_params=_cparams("parallel", "arbitrary"),
    )(sinks, pc, kn, kn, pc, pc, pc, jnp.tile(gq.reshape(1, -1), (1, 2)), bias)


def _swa_step_kernel(sink_ref, q_ref, kc_ref, kn_ref, vc_ref, vn_ref, g_ref, gq_ref, bc_ref, bn_ref, o_ref):
    _swa_heads(q_ref, g_ref, gq_ref, sink_ref, o_ref, [kc_ref[...], kn_ref[...]], [vc_ref[...], vn_ref[...]],
               [bc_ref, bn_ref], None)


def swa_attn_step(pc, k_cache, kn, v_cache, gq, sinks, bias_c, bias_n, *, bsz, t):
    dm = SWA_HEADS * SWA_HEAD_DIM
    const3 = lambda b: (0, 0, 0)
    row = lambda b: (b, 0)
    return pl.pallas_call(
        _swa_step_kernel, name="swa_attn_step",
        grid=(bsz,),
        in_specs=[pl.BlockSpec(memory_space=pltpu.SMEM),
                  pl.BlockSpec((t, dm), lambda b: (b, C_Q // dm)),
                  pl.BlockSpec((WINDOW, KVW), row), pl.BlockSpec((t, KVW), row),
                  pl.BlockSpec((WINDOW, KVW), row),
                  pl.BlockSpec((t, KVW), lambda b: (b, C_V // KVW)),
                  pl.BlockSpec((t, dm), lambda b: (b, C_G // dm)),
                  pl.BlockSpec((1, SWA_HEAD_DIM), lambda b: (0, 0)),
                  pl.BlockSpec(bias_c.shape, const3), pl.BlockSpec(bias_n.shape, const3)],
        out_specs=pl.BlockSpec((t, dm), row),
        out_shape=jax.ShapeDtypeStruct((bsz * t, dm), BF16),
        compiler_params=_cparams("parallel"),
    )(sinks, pc, k_cache, kn, v_cache, pc, pc, gq.reshape(1, -1), bias_c, bias_n)


def _rope_tables(pos, reps=1):
    half = MLA_ROPE // 2
    inv = ROPE_THETA ** (-jnp.arange(half, dtype=F32) / half)
    ang = pos.astype(F32)[:, None] * inv
    cos, sin = jnp.cos(ang), jnp.sin(ang)
    z = jnp.zeros_like(cos)
    tabs = (jnp.concatenate([cos, cos, z, z], -1), jnp.concatenate([-sin, z, z, z], -1),
            jnp.concatenate([z, sin, z, z], -1))
    return tuple(jnp.tile(t, (reps, 1)) for t in tabs)


def _rel_bucket(rel):
    nb = REL_BUCKETS // 2
    max_exact = nb // 2
    ret = jnp.where(rel > 0, nb, 0)
    n = jnp.abs(rel)
    nf = jnp.maximum(n, 1).astype(F32)
    large = max_exact + (jnp.log(nf / max_exact) / math.log(REL_MAX_DIST / max_exact)
                         * (nb - max_exact)).astype(jnp.int32)
    large = jnp.minimum(large, nb - 1)
    return ret + jnp.where(n < max_exact, n, large)


def _rel_bias(table, rel, mask):
    onehot = (_rel_bucket(rel)[:, :, None] == jnp.arange(REL_BUCKETS)).astype(F32)
    b = jnp.einsum("qkb,bh->hqk", onehot, table.astype(F32), precision=lax.Precision.HIGHEST)
    return jnp.where(mask[None], b, NEG)


def _prep_ab_weights(w_in, w_uq, w_ukv, gq, gk, w_glu, w_out):
    d = w_in.shape[0]
    wb = w_in.astype(BF16)
    w_in_r = jnp.concatenate([wb[:, 0:768], wb[:, 1280:1344], jnp.zeros((d, AB_CKV - AB_KR - MLA_ROPE), BF16),
                              wb[:, 768:1280], wb[:, 1344:]], axis=1)
    uq = w_uq.reshape(w_uq.shape[0], MLA_HEADS, MLA_QK)
    uq = jnp.pad(uq, ((0, 0), (0, 0), (0, MLA_PAD - MLA_QK))).reshape(w_uq.shape[0], -1).astype(BF16)
    ukv = w_ukv.reshape(w_ukv.shape[0], MLA_HEADS, MLA_NOPE + MLA_V)
    wn = ukv[:, :, :MLA_NOPE].reshape(w_ukv.shape[0], -1).astype(BF16)
    wv = ukv[:, :, MLA_NOPE:].reshape(w_ukv.shape[0], -1).astype(BF16)
    pad_r = lambda v: jnp.pad(v[MLA_NOPE:], (0, LANE - MLA_ROPE)).reshape(1, LANE)
    half = w_glu.shape[1] // 2
    return dict(w_in=w_in_r, w_uq=uq, wn=wn, wv=wv,
                gqn=gq[:MLA_NOPE].reshape(1, LANE), gqr=pad_r(gq),
                gkn=gk[:MLA_NOPE].reshape(1, LANE), gkr=pad_r(gk),
                wa=w_glu[:, :half].astype(BF16), wb=w_glu[:, half:].astype(BF16),
                w_out=w_out.astype(BF16))


def _prep_s5(lam_re, lam_im, log_dt, b_re, b_im, c_re, c_im):
    g = lam_re.shape[0]
    lb_re, lb_im, bbt_re, bbt_im = s5_discretize(lam_re, lam_im, log_dt, b_re, b_im)
    eye = jnp.eye(SSM_SUB, dtype=F32)
    ns = g // SSM_SUB
    bb = jnp.stack([bbt_re, bbt_im]).reshape(2, ns, SSM_SUB, SSM_GROUP, SSM_STATE)
    wbu = bb[:, :, :, :, None, :] * eye[None, None, :, None, :, None]
    wbu = jnp.transpose(wbu, (1, 2, 3, 0, 4, 5)).reshape(ns, SSM_SUB * SSM_GROUP, 2 * SSM_SUB * SSM_STATE)

    def readout(c):
        c4 = c.astype(F32).reshape(ns, SSM_SUB, SSM_GROUP, SSM_STATE)
        w = c4[:, :, :, None, :] * eye[None, :, None, :, None]
        return jnp.transpose(w, (0, 3, 4, 1, 2)).reshape(ns, SSM_SUB * SSM_STATE, SSM_SUB * SSM_GROUP)

    return dict(wbu=wbu.astype(BF16), lb_re=lb_re.reshape(1, -1), lb_im=lb_im.reshape(1, -1),
                wc_re=readout(c_re).astype(BF16), wc_im=readout(-c_im).astype(BF16))


def _ab_layer(x, w, s5w, norm_g, q_lora_g, kv_lora_g, d_skip, tabs, *, bsz, seq, past):
    m = x.shape[0]
    p = norm_matmul(x, norm_g, w["w_in"], tn=768)
    q = mla_q_prep(p, q_lora_g, w["w_uq"], tabs, w["gqn"], w["gqr"])
    lat, krp, kro = mla_latent(p, kv_lora_g, tabs)
    k, v = mla_kv_expand(lat, krp, w["wn"], w["wv"], w["gkn"], w["gkr"])
    p3 = p.reshape(bsz, seq, AB_COLS)
    gdim = s5w["lb_re"].shape[1]
    if past is None:
        a_out = mla_attn_prompt(q, k, v, p, bsz=bsz, seq=seq)
        h0 = jnp.zeros((bsz, gdim), F32)
        y, h_re, h_im = s5_mixer(p3, h0, h0, s5w["wbu"], s5w["lb_re"], s5w["lb_im"], s5w["wc_re"], s5w["wc_im"],
                                 d_skip.reshape(1, -1), tc=CHUNK)
    else:
        lat_c, kr_c, h0_re, h0_im = past
        plen = lat_c.shape[1]
        kc, vc = mla_kv_expand(lat_c.reshape(bsz * plen, -1),
                               jnp.pad(kr_c.reshape(bsz * plen, -1), ((0, 0), (0, LANE - MLA_ROPE))),
                               w["wn"], w["wv"], w["gkn"], w["gkr"])
        a_out = mla_attn_step(q, kc, vc, k, v, p, bsz=bsz, t=seq, past=plen)
        y, h_re, h_im = s5_mixer(p3, h0_re.reshape(bsz, gdim), h0_im.reshape(bsz, gdim), s5w["wbu"], s5w["lb_re"],
                                 s5w["lb_im"], s5w["wc_re"], s5w["wc_im"], d_skip.reshape(1, -1), tc=seq)
    b_out = glu_gate(y.reshape(m, -1), w["wa"], w["wb"], p)
    x_new = out_proj([a_out, b_out], w["w_out"], x)
    return x_new, (lat, kro, h_re, h_im)


def _c_layer(x, w_in, w_out, norm_g, gq, gk, sinks, biases, *, bsz, seq, past):
    pc = norm_matmul(x, norm_g, w_in, tn=768)
    kn = swa_k_norm(pc, gk)
    v = pc[:, C_V:C_V + KVW]
    if past is None:
        a = swa_attn_prompt(pc, kn, gq, sinks, *biases, bsz=bsz, seq=seq)
        k_buf = kn.reshape(bsz, seq, KVW)[:, -WINDOW:]
        v_buf = v.reshape(bsz, seq, KVW)[:, -WINDOW:]
    else:
        k_c = past[0].reshape(bsz * WINDOW, KVW)
        v_c = past[1].reshape(bsz * WINDOW, KVW)
        a = swa_attn_step(pc, k_c, kn, v_c, gq, sinks, *biases, bsz=bsz, t=seq)
        k_buf = jnp.concatenate([k_c.reshape(bsz, WINDOW, KVW), kn.reshape(bsz, seq, KVW)], axis=1)[:, -WINDOW:]
        v_buf = jnp.concatenate([v_c.reshape(bsz, WINDOW, KVW), v.reshape(bsz, seq, KVW)], axis=1)[:, -WINDOW:]
    x_new = out_proj([a], w_out, x)
    shp = (bsz, WINDOW, SWA_KV_HEADS, SWA_HEAD_DIM)
    return x_new, (k_buf.reshape(shp), v_buf.reshape(shp))


def kernel(x_prompt, x_sample, cache_mla_latent, cache_mla_krope, state_ssm_re, state_ssm_im, cache_swa_k, cache_swa_v, rel_bias_table, ab_norm, ab_w_in, ab_q_lora_norm, ab_kv_lora_norm, ab_w_uq, ab_w_ukv, ab_q_norm, ab_k_norm, ssm_lambda_re, ssm_lambda_im, ssm_b_re, ssm_b_im, ssm_c_re, ssm_c_im, ssm_log_dt, ssm_d, ssm_w_glu, ab_w_out, c_norm, c_w_in, c_q_norm, c_k_norm, c_sinks, c_w_out):
    bp, lp, d = x_prompt.shape
    bs, ts, _ = x_sample.shape
    past_len = cache_mla_latent.shape[2]
    depth = ab_norm.shape[0] + c_norm.shape[0]
    assert lp % 256 == 0 and (bs * ts) % 128 == 0 and past_len % 512 == 0 and d == SWA_HEADS * SWA_HEAD_DIM

    q_pos = past_len + np.arange(ts)
    assert np.all((np.arange(past_len + ts)[None, :] // CHUNK) <= (q_pos[:, None] // CHUNK))
    k_pos = past_len - WINDOW + np.arange(WINDOW + ts)
    qch, kch = q_pos[:, None] // CHUNK, k_pos[None, :] // CHUNK
    assert np.all((kch <= qch) & (kch >= qch - WINDOW // CHUNK))

    tabs_p = _rope_tables(jnp.arange(lp))
    tabs_s = _rope_tables(past_len + jnp.arange(ts), reps=bs)

    qi = np.arange(WINDOW)[:, None]
    kj = np.arange(2 * WINDOW)[None, :]
    band = (kj // CHUNK - 2 <= qi // CHUNK) & (kj // CHUNK >= qi // CHUNK)
    bias_p = _rel_bias(rel_bias_table, jnp.asarray(kj - WINDOW - qi), jnp.asarray(band))
    biases_p = (jnp.stack([jnp.where(jnp.asarray(kj < WINDOW)[None], NEG, bias_p), bias_p]),)
    bias_s = _rel_bias(rel_bias_table, jnp.asarray(k_pos[None, :] - q_pos[:, None]),
                       jnp.ones((ts, WINDOW + ts), bool))
    biases_s = (bias_s[:, :, :WINDOW], bias_s[:, :, WINDOW:])

    xp = x_prompt.reshape(bp * lp, d)
    xs = x_sample.reshape(bs * ts, d)
    outs_p = [[] for _ in range(6)]
    outs_s = [[] for _ in range(6)]
    for layer in range(depth):
        i = layer // 2
        if layer % 2 == 0:
            w = _prep_ab_weights(ab_w_in[i], ab_w_uq[i], ab_w_ukv[i], ab_q_norm[i], ab_k_norm[i],
                                 ssm_w_glu[i], ab_w_out[i])
            s5w = _prep_s5(ssm_lambda_re[i], ssm_lambda_im[i], ssm_log_dt[i], ssm_b_re[i], ssm_b_im[i],
                           ssm_c_re[i], ssm_c_im[i])
            args = (w, s5w, ab_norm[i], ab_q_lora_norm[i], ab_kv_lora_norm[i], ssm_d[i])
            xp, (a, b, c, e) = _ab_layer(xp, *args, tabs_p, bsz=bp, seq=lp, past=None)
            outs_p[0].append(a.reshape(bp, lp, -1)); outs_p[1].append(b.reshape(bp, lp, -1))
            outs_p[2].append(c.reshape(bp, -1, SSM_STATE)); outs_p[3].append(e.reshape(bp, -1, SSM_STATE))
            past = (cache_mla_latent[i], cache_mla_krope[i], state_ssm_re[i], state_ssm_im[i])
            xs, (a, b, c, e) = _ab_layer(xs, *args, tabs_s, bsz=bs, seq=ts, past=past)
            outs_s[0].append(a.reshape(bs, ts, -1)); outs_s[1].append(b.reshape(bs, ts, -1))
            outs_s[2].append(c.reshape(bs, -1, SSM_STATE)); outs_s[3].append(e.reshape(bs, -1, SSM_STATE))
        else:
            cw = c_w_in[i].astype(BF16)
            w_in = jnp.concatenate([cw[:, :d], cw[:, d + 2 * KVW:], cw[:, d:d + 2 * KVW]], axis=1)
            w_out = c_w_out[i].astype(BF16)
            args = (w_in, w_out, c_norm[i], c_q_norm[i], c_k_norm[i], c_sinks[i])
            xp, (a, b) = _c_layer(xp, *args, biases_p, bsz=bp, seq=lp, past=None)
            outs_p[4].append(a); outs_p[5].append(b)
            xs, (a, b) = _c_layer(xs, *args, biases_s, bsz=bs, seq=ts, past=(cache_swa_k[i], cache_swa_v[i]))
            outs_s[4].append(a); outs_s[5].append(b)
    return (xp.reshape(bp, lp, d), xs.reshape(bs, ts, d),
            *[jnp.stack(o) for o in outs_p], *[jnp.stack(o) for o in outs_s])
```

```python
import functools
import math

import jax
import jax.numpy as jnp
import numpy as np
from jax import lax
from jax.experimental import pallas as pl
from jax.experimental.pallas import tpu as pltpu

F32 = jnp.float32
BF16 = jnp.bfloat16

EPS = 1e-6
CHUNK = 64
ROPE_THETA = 10000.0
MLA_HEADS = 16
MLA_NOPE = 128
MLA_ROPE = 64
MLA_QK = MLA_NOPE + MLA_ROPE
MLA_V = 128
MLA_PAD = 256
SSM_GROUP = 16
SSM_STATE = 64
SSM_SUB = 8
SWA_HEADS = 64
SWA_KV_HEADS = 8
SWA_GQ = SWA_HEADS // SWA_KV_HEADS
SWA_HEAD_DIM = 64
WINDOW = 128
REL_BUCKETS = 32
REL_MAX_DIST = 128
NEG = -1e30
LANE = 128
VMEM_LIMIT = 56 * 1024 * 1024

AB_CQ = 0
AB_KR = 768
AB_CKV = 1024
AB_GA = 1536
AB_U = 3584
AB_GB = 5632
AB_COLS = 7680


def _cparams(*sem):
    return pltpu.CompilerParams(dimension_semantics=sem, vmem_limit_bytes=VMEM_LIMIT)


def _rms(x, g):
    return x * lax.rsqrt(jnp.mean(x * x, axis=-1, keepdims=True) + EPS) * g


def _silu(x):
    return x * jax.nn.sigmoid(x)


def _gelu_tanh(x):
    return 0.5 * x * (1.0 + jnp.tanh(math.sqrt(2.0 / math.pi) * (x + 0.044715 * (x * x * x))))


def _pick(n, prefs):
    for p in prefs:
        if n % p == 0:
            return p
    raise ValueError(f"no tile for {n} in {prefs}")


def _norm_mm_kernel(x_ref, g_ref, w_ref, o_ref, h_ref):
    @pl.when(pl.program_id(1) == 0)
    def _():
        h_ref[...] = _rms(x_ref[...], g_ref[...]).astype(BF16)

    o_ref[...] = jnp.dot(h_ref[...], w_ref[...], preferred_element_type=F32).astype(o_ref.dtype)


def norm_matmul(x, g, w, *, tn, out_dtype=F32):
    m, k = x.shape
    n = w.shape[1]
    tm = _pick(m, (512, 256, 128))
    return pl.pallas_call(
        _norm_mm_kernel, name="norm_mm",
        grid=(m // tm, n // tn),
        in_specs=[pl.BlockSpec((tm, k), lambda i, j: (i, 0)),
                  pl.BlockSpec((1, k), lambda i, j: (0, 0)),
                  pl.BlockSpec((k, tn), lambda i, j: (0, j))],
        out_specs=pl.BlockSpec((tm, tn), lambda i, j: (i, j)),
        out_shape=jax.ShapeDtypeStruct((m, n), out_dtype),
        scratch_shapes=[pltpu.VMEM((tm, k), BF16)],
        compiler_params=_cparams("parallel", "arbitrary"),
    )(x, g.reshape(1, k), w)


def _out_proj_kernel(*refs, n_lhs):
    lhs = refs[:n_lhs]
    w_ref, x_ref, o_ref = refs[n_lhs:]
    acc = x_ref[...]
    off = 0
    for a in lhs:
        kk = a.shape[1]
        acc = acc + jnp.dot(a[...], w_ref[off:off + kk, :], preferred_element_type=F32)
        off += kk
    o_ref[...] = acc


def out_proj(lhs, w, x, *, tn=512):
    m, n = x.shape
    k = w.shape[0]
    tm = _pick(m, (1024, 512, 256, 128))
    in_specs = [pl.BlockSpec((tm, a.shape[1]), lambda i, j: (i, 0)) for a in lhs]
    in_specs += [pl.BlockSpec((k, tn), lambda i, j: (0, j)),
                 pl.BlockSpec((tm, tn), lambda i, j: (i, j))]
    return pl.pallas_call(
        functools.partial(_out_proj_kernel, n_lhs=len(lhs)), name="out_proj",
        grid=(m // tm, n // tn),
        in_specs=in_specs,
        out_specs=pl.BlockSpec((tm, tn), lambda i, j: (i, j)),
        out_shape=jax.ShapeDtypeStruct((m, n), F32),
        compiler_params=_cparams("parallel", "arbitrary"),
    )(*lhs, w, x)


def _rope_slab(r, c, s1, s2):
    return r * c + pltpu.roll(r, 96, 1) * s1 + pltpu.roll(r, 32, 1) * s2


def _q_prep_kernel(cq_ref, g_ref, w_ref, c_ref, s1_ref, s2_ref, gn_ref, gr_ref, q_ref):
    h = _rms(cq_ref[...], g_ref[...]).astype(BF16)
    c, s1, s2 = c_ref[...], s1_ref[...], s2_ref[...]
    scale = MLA_QK ** -0.5 * math.log2(math.e)
    gn = gn_ref[...] * scale
    gr = gr_ref[...] * scale
    for hd in range(MLA_HEADS):
        lo = hd * MLA_PAD
        nope = jnp.dot(h, w_ref[:, lo:lo + LANE], preferred_element_type=F32)
        rr = _rope_slab(jnp.dot(h, w_ref[:, lo + LANE:lo + MLA_PAD], preferred_element_type=F32), c, s1, s2)
        ss = jnp.sum(nope * nope, axis=-1, keepdims=True) + jnp.sum(rr * rr, axis=-1, keepdims=True)
        rs = lax.rsqrt(ss * (1.0 / MLA_QK) + EPS)
        q_ref[:, lo:lo + LANE] = (nope * rs * gn).astype(BF16)
        q_ref[:, lo + LANE:lo + MLA_PAD] = (rr * rs * gr).astype(BF16)


def mla_q_prep(p, g, w_uq, tabs, gn, gr):
    m = p.shape[0]
    period = tabs[0].shape[0]
    tm = _pick(period, (256, 128))
    nb = period // tm
    tab_spec = pl.BlockSpec((tm, LANE), lambda i: (i % nb, 0))
    vec = lambda n: pl.BlockSpec((1, n), lambda i: (0, 0))
    return pl.pallas_call(
        _q_prep_kernel, name="mla_q_prep",
        grid=(m // tm,),
        in_specs=[pl.BlockSpec((tm, 768), lambda i: (i, 0)), vec(768),
                  pl.BlockSpec(w_uq.shape, lambda i: (0, 0)),
                  tab_spec, tab_spec, tab_spec, vec(LANE), vec(LANE)],
        out_specs=pl.BlockSpec((tm, MLA_HEADS * MLA_PAD), lambda i: (i, 0)),
        out_shape=jax.ShapeDtypeStruct((m, MLA_HEADS * MLA_PAD), BF16),
        compiler_params=_cparams("parallel"),
    )(p, g.reshape(1, -1), w_uq, *tabs, gn, gr)


def _latent_kernel(ckv_ref, kr_ref, g_ref, c_ref, s1_ref, s2_ref, lat_ref, krp_ref, kro_ref):
    lat_ref[...] = _rms(ckv_ref[...], g_ref[...])
    rr = _rope_slab(kr_ref[...], c_ref[...], s1_ref[...], s2_ref[...])
    krp_ref[...] = rr
    kro_ref[...] = rr[:, :MLA_ROPE]


def mla_latent(p, g, tabs):
    m = p.shape[0]
    period = tabs[0].shape[0]
    tm = _pick(period, (512, 256, 128))
    nb = period // tm
    tab_spec = pl.BlockSpec((tm, LANE), lambda i: (i % nb, 0))
    return pl.pallas_call(
        _latent_kernel, name="mla_latent",
        grid=(m // tm,),
        in_specs=[pl.BlockSpec((tm, 512), lambda i: (i, AB_CKV // 512)),
                  pl.BlockSpec((tm, LANE), lambda i: (i, AB_KR // LANE)),
                  pl.BlockSpec((1, 512), lambda i: (0, 0)),
                  tab_spec, tab_spec, tab_spec],
        out_specs=[pl.BlockSpec((tm, 512), lambda i: (i, 0)),
                   pl.BlockSpec((tm, LANE), lambda i: (i, 0)),
                   pl.BlockSpec((tm, MLA_ROPE), lambda i: (i, 0))],
        out_shape=[jax.ShapeDtypeStruct((m, 512), F32),
                   jax.ShapeDtypeStruct((m, LANE), F32),
                   jax.ShapeDtypeStruct((m, MLA_ROPE), F32)],
        compiler_params=_cparams("parallel"),
    )(p, p, g.reshape(1, -1), *tabs)


def _kv_expand_kernel(lat_ref, krp_ref, wn_ref, wv_ref, gn_ref, gr_ref, k_ref, v_ref):
    lat = lat_ref[...].astype(BF16)
    kr = krp_ref[...]
    ss_r = jnp.sum(kr * kr, axis=-1, keepdims=True)
    gn, gr = gn_ref[...], gr_ref[...]
    for hd in range(MLA_HEADS):
        kn = jnp.dot(lat, wn_ref[:, hd * LANE:(hd + 1) * LANE], preferred_element_type=F32)
        rs = lax.rsqrt((jnp.sum(kn * kn, axis=-1, keepdims=True) + ss_r) * (1.0 / MLA_QK) + EPS)
        lo = hd * MLA_PAD
        k_ref[:, lo:lo + LANE] = (kn * rs * gn).astype(BF16)
        k_ref[:, lo + LANE:lo + MLA_PAD] = (kr * rs * gr).astype(BF16)
        v_ref[:, hd * LANE:(hd + 1) * LANE] = jnp.dot(
            lat, wv_ref[:, hd * LANE:(hd + 1) * LANE], preferred_element_type=F32).astype(BF16)


def mla_kv_expand(lat, krp, wn, wv, gn, gr):
    m = lat.shape[0]
    tm = _pick(m, (512, 256, 128))
    vec = pl.BlockSpec((1, LANE), lambda i: (0, 0))
    return pl.pallas_call(
        _kv_expand_kernel, name="mla_kv_expand",
        grid=(m // tm,),
        in_specs=[pl.BlockSpec((tm, 512), lambda i: (i, 0)),
                  pl.BlockSpec((tm, LANE), lambda i: (i, 0)),
                  pl.BlockSpec(wn.shape, lambda i: (0, 0)),
                  pl.BlockSpec(wv.shape, lambda i: (0, 0)), vec, vec],
        out_specs=[pl.BlockSpec((tm, MLA_HEADS * MLA_PAD), lambda i: (i, 0)),
                   pl.BlockSpec((tm, MLA_HEADS * MLA_V), lambda i: (i, 0))],
        out_shape=[jax.ShapeDtypeStruct((m, MLA_HEADS * MLA_PAD), BF16),
                   jax.ShapeDtypeStruct((m, MLA_HEADS * MLA_V), BF16)],
        compiler_params=_cparams("parallel"),
    )(lat, krp, wn, wv, gn, gr)


def _softmax_step(q, kblk, vblk, carry, mask=None):
    m, l, acc = carry
    s = lax.dot_general(q, kblk, (((1,), (1,)), ((), ())), preferred_element_type=F32)
    if mask is not None:
        s = jnp.where(mask, s, NEG)
    m_new = jnp.maximum(m, jnp.max(s, axis=-1, keepdims=True))
    p = jnp.exp2(s - m_new)
    alpha = jnp.exp2(m - m_new)
    l = alpha * l + jnp.sum(p, axis=-1, keepdims=True)
    acc = alpha * acc + jnp.dot(p.astype(BF16), vblk, preferred_element_type=F32)
    return m_new, l, acc


def _mla_attn_kernel(q_ref, k_ref, v_ref, g_ref, o_ref, *, tq):
    i = pl.program_id(2)
    q = q_ref[...]

    def body(j, carry):
        off = pl.multiple_of(j * tq, tq)
        return _softmax_step(q, k_ref[pl.ds(off, tq), :], v_ref[pl.ds(off, tq), :], carry)

    init = (jnp.full((tq, 1), NEG, F32), jnp.zeros((tq, 1), F32), jnp.zeros((tq, MLA_V), F32))
    carry = lax.fori_loop(0, i, body, init)
    off = pl.multiple_of(i * tq, tq)
    row = lax.broadcasted_iota(jnp.int32, (tq, tq), 0) // CHUNK
    col = lax.broadcasted_iota(jnp.int32, (tq, tq), 1) // CHUNK
    _, l, acc = _softmax_step(q, k_ref[pl.ds(off, tq), :], v_ref[pl.ds(off, tq), :], carry, col <= row)
    o_ref[...] = (acc / l * _silu(g_ref[...])).astype(o_ref.dtype)


def mla_attn_prompt(q, k, v, p, *, bsz, seq):
    tq = _pick(seq, (512, 256))
    nq = seq // tq
    return pl.pallas_call(
        functools.partial(_mla_attn_kernel, tq=tq), name="mla_attn",
        grid=(bsz, MLA_HEADS, nq),
        in_specs=[pl.BlockSpec((tq, MLA_PAD), lambda b, h, i: (b * nq + i, h)),
                  pl.BlockSpec((seq, MLA_PAD), lambda b, h, i: (b, h)),
                  pl.BlockSpec((seq, MLA_V), lambda b, h, i: (b, h)),
                  pl.BlockSpec((tq, MLA_V), lambda b, h, i: (b * nq + i, AB_GA // MLA_V + h))],
        out_specs=pl.BlockSpec((tq, MLA_V), lambda b, h, i: (b * nq + i, h)),
        out_shape=jax.ShapeDtypeStruct((bsz * seq, MLA_HEADS * MLA_V), BF16),
        compiler_params=_cparams("parallel", "parallel", "arbitrary"),
    )(q, k, v, p)


def _mla_attn_step_kernel(q_ref, kc_ref, vc_ref, kn_ref, vn_ref, g_ref, o_ref, *, heads):
    t = q_ref.shape[0]
    init = (jnp.full((t, 1), NEG, F32), jnp.zeros((t, 1), F32), jnp.zeros((t, MLA_V), F32))
    for hd in range(heads):
        ks = slice(hd * MLA_PAD, (hd + 1) * MLA_PAD)
        vs = slice(hd * MLA_V, (hd + 1) * MLA_V)
        q = q_ref[:, ks]
        carry = _softmax_step(q, kc_ref[:, ks], vc_ref[:, vs], init)
        _, l, acc = _softmax_step(q, kn_ref[:, ks], vn_ref[:, vs], carry)
        o_ref[:, vs] = (acc / l * _silu(g_ref[:, vs])).astype(o_ref.dtype)


def mla_attn_step(q, kc, vc, kn, vn, p, *, bsz, t, past):
    heads = 4
    kw, vw = heads * MLA_PAD, heads * MLA_V
    return pl.pallas_call(
        functools.partial(_mla_attn_step_kernel, heads=heads), name="mla_attn_step",
        grid=(bsz, MLA_HEADS // heads),
        in_specs=[pl.BlockSpec((t, kw), lambda b, h: (b, h)),
                  pl.BlockSpec((past, kw), lambda b, h: (b, h)),
                  pl.BlockSpec((past, vw), lambda b, h: (b, h)),
                  pl.BlockSpec((t, kw), lambda b, h: (b, h)),
                  pl.BlockSpec((t, vw), lambda b, h: (b, h)),
                  pl.BlockSpec((t, vw), lambda b, h: (b, AB_GA // vw + h))],
        out_specs=pl.BlockSpec((t, vw), lambda b, h: (b, h)),
        out_shape=jax.ShapeDtypeStruct((bsz * t, MLA_HEADS * MLA_V), BF16),
        compiler_params=_cparams("parallel", "parallel"),
    )(q, kc, vc, kn, vn, p)


def _s5_disc_kernel(lre_ref, lim_ref, ldt_ref, bre_ref, bim_ref, lbre_ref, lbim_ref, bbre_ref, bbim_ref):
    lam_re = jnp.minimum(lre_ref[...], -1e-4)
    lam_im = lim_ref[...]
    dt = jnp.exp(ldt_ref[...])
    mag = jnp.exp(lam_re * dt)
    lb_re = mag * jnp.cos(lam_im * dt)
    lb_im = mag * jnp.sin(lam_im * dt)
    nr, ni = lb_re - 1.0, lb_im
    den = lam_re * lam_re + lam_im * lam_im
    f_re = (nr * lam_re + ni * lam_im) / den
    f_im = (ni * lam_re - nr * lam_im) / den
    b_re, b_im = bre_ref[...], bim_ref[...]
    lbre_ref[...] = lb_re
    lbim_ref[...] = lb_im
    bbre_ref[...] = f_re * b_re - f_im * b_im
    bbim_ref[...] = f_re * b_im + f_im * b_re


def s5_discretize(lam_re, lam_im, log_dt, b_re, b_im):
    g, pdim = lam_re.shape
    bt_re = jnp.swapaxes(b_re, 1, 2)
    bt_im = jnp.swapaxes(b_im, 1, 2)
    s3 = jax.ShapeDtypeStruct((g, 1, pdim), F32)
    sb = jax.ShapeDtypeStruct(bt_re.shape, F32)
    return pl.pallas_call(
        _s5_disc_kernel, name="s5_disc", out_shape=[s3, s3, sb, sb],
    )(lam_re.reshape(g, 1, pdim), lam_im.reshape(g, 1, pdim), log_dt.reshape(g, 1, 1), bt_re, bt_im)


def _s5_kernel(u_ref, h0re_ref, h0im_ref, wbu_ref, lbre_ref, lbim_ref, wcre_ref, wcim_ref, d_ref,
               y_ref, hre_ref, him_ref, ut_ref, sre_ref, sim_ref, *, bsz, tc, nsub, group, unroll):
    sub = 8
    assert bsz in (4, 8) and SSM_SUB * SSM_GROUP == LANE
    rows = bsz * tc
    sw = SSM_SUB * SSM_STATE
    uw = SSM_SUB * SSM_GROUP
    tiles = sw // LANE

    @pl.when(pl.program_id(1) == 0)
    def _():
        hre_ref[...] = h0re_ref[...]
        him_ref[...] = h0im_ref[...]

    for b in range(bsz):
        for s in range(nsub):
            ut_ref[s, pl.ds(b, tc, stride=bsz), :] = u_ref[b, :, s * uw:(s + 1) * uw]

    for s in range(nsub):
        bu = jnp.dot(ut_ref[s].astype(BF16), wbu_ref[s], preferred_element_type=F32)
        for k in range(tiles):
            sre_ref[s * tiles + k] = bu[:, k * LANE:(k + 1) * LANE]
            sim_ref[s * tiles + k] = bu[:, sw + k * LANE:sw + (k + 1) * LANE]

    low = lax.broadcasted_iota(jnp.int32, (sub, LANE), 0) < bsz
    for c0 in range(0, nsub * tiles, group):
        lanes = [slice((c0 + k) * LANE, (c0 + k + 1) * LANE) for k in range(group)]
        ar = [jnp.broadcast_to(lbre_ref[:, ls], (sub, LANE)) for ls in lanes]
        ai = [jnp.broadcast_to(lbim_ref[:, ls], (sub, LANE)) for ls in lanes]

        def step(i, carry, c0=c0, ar=ar, ai=ai):
            rs = pl.ds(pl.multiple_of(i * sub, sub), sub)
            out = []
            for k in range(group):
                hr, hi = carry[2 * k], carry[2 * k + 1]
                xr, xi = sre_ref[c0 + k, rs, :], sim_ref[c0 + k, rs, :]
                if bsz == sub:
                    nr = ar[k] * hr - ai[k] * hi + xr
                    ni = ar[k] * hi + ai[k] * hr + xi
                    sr, si = nr, ni
                else:
                    pr, pi = pltpu.roll(hr, bsz, 0), pltpu.roll(hi, bsz, 0)
                    v1r = ar[k] * pr - ai[k] * pi + xr
                    v1i = ar[k] * pi + ai[k] * pr + xi
                    qr, qi = pltpu.roll(v1r, bsz, 0), pltpu.roll(v1i, bsz, 0)
                    nr = ar[k] * qr - ai[k] * qi + xr
                    ni = ar[k] * qi + ai[k] * qr + xi
                    sr, si = jnp.where(low, v1r, nr), jnp.where(low, v1i, ni)
                sre_ref[c0 + k, rs, :] = sr
                sim_ref[c0 + k, rs, :] = si
                out += [nr, ni]
            return tuple(out)

        def widen(x):
            return x if bsz == sub else jnp.concatenate([x, x], axis=0)

        init = []
        for ls in lanes:
            init += [widen(hre_ref[:, ls]), widen(him_ref[:, ls])]
        fin = lax.fori_loop(0, rows // sub, step, tuple(init), unroll=unroll)
        for k, ls in enumerate(lanes):
            hre_ref[:, ls] = fin[2 * k][sub - bsz:]
            him_ref[:, ls] = fin[2 * k + 1][sub - bsz:]

    d = d_ref[...]
    for s in range(nsub):
        hre = jnp.concatenate([sre_ref[s * tiles + k] for k in range(tiles)], axis=-1).astype(BF16)
        him = jnp.concatenate([sim_ref[s * tiles + k] for k in range(tiles)], axis=-1).astype(BF16)
        y = jnp.dot(hre, wcre_ref[s], preferred_element_type=F32)
        y = y + jnp.dot(him, wcim_ref[s], preferred_element_type=F32)
        y = y + d[:, s * uw:(s + 1) * uw] * ut_ref[s]
        ut_ref[s] = _gelu_tanh(y)
    for b in range(bsz):
        for s in range(nsub):
            y_ref[b, :, s * uw:(s + 1) * uw] = ut_ref[s, pl.ds(b, tc, stride=bsz), :].astype(BF16)


def s5_mixer(p3, h0_re, h0_im, wbu, lb_re, lb_im, wc_re, wc_im, d_skip, *, tc):
    bsz, t, _ = p3.shape
    nsub = 4
    uw = nsub * SSM_SUB * SSM_GROUP
    sw = nsub * SSM_SUB * SSM_STATE
    nblk = wbu.shape[0] // nsub
    bw = uw * nblk
    kern = functools.partial(_s5_kernel, bsz=bsz, tc=tc, nsub=nsub, group=4, unroll=4)
    st_spec = pl.BlockSpec((bsz, sw), lambda g, c: (0, g))
    return pl.pallas_call(
        kern, name="s5_mixer",
        grid=(nblk, t // tc),
        in_specs=[pl.BlockSpec((bsz, tc, uw), lambda g, c: (0, c, AB_U // uw + g)),
                  st_spec, st_spec,
                  pl.BlockSpec((nsub,) + wbu.shape[1:], lambda g, c: (g, 0, 0)),
                  pl.BlockSpec((1, sw), lambda g, c: (0, g)),
                  pl.BlockSpec((1, sw), lambda g, c: (0, g)),
                  pl.BlockSpec((nsub,) + wc_re.shape[1:], lambda g, c: (g, 0, 0)),
                  pl.BlockSpec((nsub,) + wc_im.shape[1:], lambda g, c: (g, 0, 0)),
                  pl.BlockSpec((1, uw), lambda g, c: (0, g))],
        out_specs=[pl.BlockSpec((bsz, tc, uw), lambda g, c: (0, c, g)), st_spec, st_spec],
        out_shape=[jax.ShapeDtypeStruct((bsz, t, bw), BF16),
                   jax.ShapeDtypeStruct(h0_re.shape, F32),
                   jax.ShapeDtypeStruct(h0_im.shape, F32)],
        scratch_shapes=[pltpu.VMEM((nsub, bsz * tc, LANE), F32),
                        pltpu.VMEM((sw // LANE, bsz * tc, LANE), F32),
                        pltpu.VMEM((sw // LANE, bsz * tc, LANE), F32)],
        compiler_params=_cparams("parallel", "arbitrary"),
    )(p3, h0_re, h0_im, wbu, lb_re, lb_im, wc_re, wc_im, d_skip)


def _glu_kernel(y_ref, wa_ref, wb_ref, g_ref, o_ref):
    y = y_ref[...]
    ga = jnp.dot(y, wa_ref[...], preferred_element_type=F32)
    gb = jnp.dot(y, wb_ref[...], preferred_element_type=F32)
    o_ref[...] = (ga * jax.nn.sigmoid(gb) * _silu(g_ref[...])).astype(o_ref.dtype)


def glu_gate(y, wa, wb, p, *, tn=512):
    m, k = y.shape
    n = wa.shape[1]
    tm = _pick(m, (1024, 512, 256, 128))
    return pl.pallas_call(
        _glu_kernel, name="glu_gate",
        grid=(m // tm, n // tn),
        in_specs=[pl.BlockSpec((tm, k), lambda i, j: (i, 0)),
                  pl.BlockSpec((k, tn), lambda i, j: (0, j)),
                  pl.BlockSpec((k, tn), lambda i, j: (0, j)),
                  pl.BlockSpec((tm, tn), lambda i, j: (i, AB_GB // tn + j))],
        out_specs=pl.BlockSpec((tm, tn), lambda i, j: (i, j)),
        out_shape=jax.ShapeDtypeStruct((m, n), BF16),
        compiler_params=_cparams("parallel", "arbitrary"),
    )(y, wa, wb, p)


C_Q = 0
C_G = 4096
C_K = 8192
C_V = 8704
KVW = SWA_KV_HEADS * SWA_HEAD_DIM


def _k_norm_kernel(k_ref, g_ref, o_ref):
    k = k_ref[...]
    g = g_ref[...]
    for h in range(SWA_KV_HEADS):
        sl = slice(h * SWA_HEAD_DIM, (h + 1) * SWA_HEAD_DIM)
        o_ref[:, sl] = _rms(k[:, sl], g)


def swa_k_norm(pc, gk):
    m = pc.shape[0]
    tm = _pick(m, (512, 256, 128))
    return pl.pallas_call(
        _k_norm_kernel, name="swa_k_norm",
        grid=(m // tm,),
        in_specs=[pl.BlockSpec((tm, KVW), lambda i: (i, C_K // KVW)),
                  pl.BlockSpec((1, SWA_HEAD_DIM), lambda i: (0, 0))],
        out_specs=pl.BlockSpec((tm, KVW), lambda i: (i, 0)),
        out_shape=jax.ShapeDtypeStruct((m, KVW), F32),
        compiler_params=_cparams("parallel"),
    )(pc, gk.reshape(1, -1))


def _swa_prompt_kernel(sink_ref, q_ref, kp_ref, kc_ref, vp_ref, vc_ref, g_ref, gq_ref, b_ref, o_ref, *, pair_batch):
    nk = b_ref.shape[-1]
    pad = nk - kp_ref.shape[0] - kc_ref.shape[0]
    fill = [jnp.zeros((pad, KVW), F32)] if pad else []
    kcat = jnp.concatenate([kp_ref[...], kc_ref[...]] + fill, axis=0)
    vcat = jnp.concatenate([vp_ref[...], vc_ref[...]] + fill, axis=0)
    seg_r = lax.broadcasted_iota(jnp.int32, (LANE, LANE), 0) // SWA_HEAD_DIM
    seg_c = lax.broadcasted_iota(jnp.int32, (LANE, LANE), 1) // SWA_HEAD_DIM
    seg_ones = (seg_r == seg_c).astype(BF16)
    left = lax.broadcasted_iota(jnp.int32, (1, LANE), 1) < SWA_HEAD_DIM
    gq = gq_ref[...] * (SWA_HEAD_DIM ** -0.5)
    nt = (((1,), (1,)), ((), ()))

    def halves(slab, half):
        own = jnp.where(left if half == 0 else jnp.logical_not(left), slab, 0.0)
        other = pltpu.roll(own, SWA_HEAD_DIM, 1)
        lo, hi = (own, other) if half == 0 else (other, own)
        return lo.astype(BF16), hi.astype(BF16)

    kv = {}

    def kv_halves(kvh):
        if kvh not in kv:
            ks = slice((kvh // 2) * LANE, (kvh // 2 + 1) * LANE)
            kv.clear()
            kv[kvh] = halves(kcat[:, ks], kvh % 2) + halves(vcat[:, ks], kvh % 2)
        return kv[kvh]

    def scores(pair):
        sl = slice(pair * LANE, (pair + 1) * LANE)
        k_lo, k_hi, _, _ = kv_halves(pair // (SWA_GQ // 2))
        q = q_ref[:, sl]
        x2 = q * q
        x2_hi = x2.astype(BF16)
        x2_lo = (x2 - x2_hi.astype(F32)).astype(BF16)
        ss = (jnp.dot(x2_hi, seg_ones, preferred_element_type=F32)
              + jnp.dot(x2_lo, seg_ones, preferred_element_type=F32))
        qn = (q * lax.rsqrt(ss * (1.0 / SWA_HEAD_DIM) + EPS) * gq).astype(BF16)
        return [lax.dot_general(qn, k_lo, nt, preferred_element_type=F32) + b_ref[2 * pair],
                lax.dot_general(qn, k_hi, nt, preferred_element_type=F32) + b_ref[2 * pair + 1]]

    def finish(pairs, ss, vhs):
        sinks = [sink_ref[2 * p + j] for p in pairs for j in range(2)]
        flat = [s for pr in ss for s in pr]
        ms = [jnp.maximum(jnp.max(s, axis=-1, keepdims=True), sk) for s, sk in zip(flat, sinks)]
        es = [jnp.exp(s - m) for s, m in zip(flat, ms)]
        ds = [jnp.sum(e, axis=-1, keepdims=True) + jnp.exp(sk - m) for e, m, sk in zip(es, ms, sinks)]
        for i, p in enumerate(pairs):
            v_lo, v_hi = vhs[i]
            o = (jnp.dot(es[2 * i].astype(BF16), v_lo, preferred_element_type=F32)
                 + jnp.dot(es[2 * i + 1].astype(BF16), v_hi, preferred_element_type=F32))
            inv = jnp.where(left, 1.0 / ds[2 * i], 1.0 / ds[2 * i + 1])
            sl = slice(p * LANE, (p + 1) * LANE)
            o_ref[:, sl] = (o * inv * _silu(g_ref[:, sl])).astype(o_ref.dtype)

    npairs = SWA_HEADS // 2
    batches = [list(range(b, b + pair_batch)) for b in range(0, npairs, pair_batch)]
    nxt = ([scores(p) for p in batches[0]], [kv_halves(p // (SWA_GQ // 2))[2:] for p in batches[0]])
    for bi, pairs in enumerate(batches):
        cur = nxt
        if bi + 1 < len(batches):
            nb = batches[bi + 1]
            nxt = ([scores(p) for p in nb], [kv_halves(p // (SWA_GQ // 2))[2:] for p in nb])
        finish(pairs, *cur)


def swa_attn_prompt(pc, kn, gq, sinks, bias, *, bsz, seq):
    tq = WINDOW
    nq = seq // tq
    dm = SWA_HEADS * SWA_HEAD_DIM
    prev = lambda b, i: (b * nq + jnp.maximum(i - 1, 0), 0)
    cur = lambda b, i: (b * nq + i, 0)
    return pl.pallas_call(
        functools.partial(_swa_prompt_kernel, pair_batch=2), name="swa_attn",
        grid=(bsz, nq),
        in_specs=[pl.BlockSpec(memory_space=pltpu.SMEM),
                  pl.BlockSpec((tq, dm), lambda b, i: (b * nq + i, C_Q // dm)),
                  pl.BlockSpec((tq, KVW), prev), pl.BlockSpec((tq, KVW), cur),
                  pl.BlockSpec((tq, KVW), lambda b, i: (b * nq + jnp.maximum(i - 1, 0), C_V // KVW)),
                  pl.BlockSpec((tq, KVW), lambda b, i: (b * nq + i, C_V // KVW)),
                  pl.BlockSpec((tq, dm), lambda b, i: (b * nq + i, C_G // dm)),
                  pl.BlockSpec((1, LANE), lambda b, i: (0, 0)),
                  pl.BlockSpec((None,) + bias.shape[1:], lambda b, i: (jnp.minimum(i, 1), 0, 0, 0))],
        out_specs=pl.BlockSpec((tq, dm), cur),
        out_shape=jax.ShapeDtypeStruct((bsz * seq, dm), BF16),
        compiler_params=_cparams("parallel", "arbitrary"),
    )(sinks, pc, kn, kn, pc, pc, pc, jnp.tile(gq.reshape(1, -1), (1, 2)), bias)


def swa_attn_step(pc, k_cache, kn, v_cache, gq, sinks, bias, *, bsz, t):
    dm = SWA_HEADS * SWA_HEAD_DIM
    row = lambda b: (b, 0)
    return pl.pallas_call(
        functools.partial(_swa_prompt_kernel, pair_batch=4), name="swa_attn_step",
        grid=(bsz,),
        in_specs=[pl.BlockSpec(memory_space=pltpu.SMEM),
                  pl.BlockSpec((t, dm), lambda b: (b, C_Q // dm)),
                  pl.BlockSpec((WINDOW, KVW), row), pl.BlockSpec((t, KVW), row),
                  pl.BlockSpec((WINDOW, KVW), row),
                  pl.BlockSpec((t, KVW), lambda b: (b, C_V // KVW)),
                  pl.BlockSpec((t, dm), lambda b: (b, C_G // dm)),
                  pl.BlockSpec((1, LANE), lambda b: (0, 0)),
                  pl.BlockSpec(bias.shape, lambda b: (0, 0, 0))],
        out_specs=pl.BlockSpec((t, dm), row),
        out_shape=jax.ShapeDtypeStruct((bsz * t, dm), BF16),
        compiler_params=_cparams("parallel"),
    )(sinks, pc, k_cache, kn, v_cache, pc, pc, jnp.tile(gq.reshape(1, -1), (1, 2)), bias)


def _rope_tables(pos, reps=1):
    half = MLA_ROPE // 2
    inv = ROPE_THETA ** (-jnp.arange(half, dtype=F32) / half)
    ang = pos.astype(F32)[:, None] * inv
    cos, sin = jnp.cos(ang), jnp.sin(ang)
    z = jnp.zeros_like(cos)
    tabs = (jnp.concatenate([cos, cos, z, z], -1), jnp.concatenate([-sin, z, z, z], -1),
            jnp.concatenate([z, sin, z, z], -1))
    return tuple(jnp.tile(t, (reps, 1)) for t in tabs)


def _rel_bucket(rel):
    nb = REL_BUCKETS // 2
    max_exact = nb // 2
    ret = jnp.where(rel > 0, nb, 0)
    n = jnp.abs(rel)
    nf = jnp.maximum(n, 1).astype(F32)
    large = max_exact + (jnp.log(nf / max_exact) / math.log(REL_MAX_DIST / max_exact)
                         * (nb - max_exact)).astype(jnp.int32)
    large = jnp.minimum(large, nb - 1)
    return ret + jnp.where(n < max_exact, n, large)


def _rel_bias(table, rel, mask):
    onehot = (_rel_bucket(rel)[:, :, None] == jnp.arange(REL_BUCKETS)).astype(F32)
    b = jnp.einsum("qkb,bh->hqk", onehot, table.astype(F32), precision=lax.Precision.HIGHEST)
    return jnp.where(mask[None], b, NEG)


def _prep_ab_weights(w_in, w_uq, w_ukv, gq, gk, w_glu, w_out):
    d = w_in.shape[0]
    wb = w_in.astype(BF16)
    w_in_r = jnp.concatenate([wb[:, 0:768], wb[:, 1280:1344], jnp.zeros((d, AB_CKV - AB_KR - MLA_ROPE), BF16),
                              wb[:, 768:1280], wb[:, 1344:]], axis=1)
    uq = w_uq.reshape(w_uq.shape[0], MLA_HEADS, MLA_QK)
    uq = jnp.pad(uq, ((0, 0), (0, 0), (0, MLA_PAD - MLA_QK))).reshape(w_uq.shape[0], -1).astype(BF16)
    ukv = w_ukv.reshape(w_ukv.shape[0], MLA_HEADS, MLA_NOPE + MLA_V)
    wn = ukv[:, :, :MLA_NOPE].reshape(w_ukv.shape[0], -1).astype(BF16)
    wv = ukv[:, :, MLA_NOPE:].reshape(w_ukv.shape[0], -1).astype(BF16)
    pad_r = lambda v: jnp.pad(v[MLA_NOPE:], (0, LANE - MLA_ROPE)).reshape(1, LANE)
    half = w_glu.shape[1] // 2
    return dict(w_in=w_in_r, w_uq=uq, wn=wn, wv=wv,
                gqn=gq[:MLA_NOPE].reshape(1, LANE), gqr=pad_r(gq),
                gkn=gk[:MLA_NOPE].reshape(1, LANE), gkr=pad_r(gk),
                wa=w_glu[:, :half].astype(BF16), wb=w_glu[:, half:].astype(BF16),
                w_out=w_out.astype(BF16))


def _prep_s5(lam_re, lam_im, log_dt, b_re, b_im, c_re, c_im):
    g = lam_re.shape[0]
    lb_re, lb_im, bbt_re, bbt_im = s5_discretize(lam_re, lam_im, log_dt, b_re, b_im)
    eye = jnp.eye(SSM_SUB, dtype=F32)
    ns = g // SSM_SUB
    bb = jnp.stack([bbt_re, bbt_im]).reshape(2, ns, SSM_SUB, SSM_GROUP, SSM_STATE)
    wbu = bb[:, :, :, :, None, :] * eye[None, None, :, None, :, None]
    wbu = jnp.transpose(wbu, (1, 2, 3, 0, 4, 5)).reshape(ns, SSM_SUB * SSM_GROUP, 2 * SSM_SUB * SSM_STATE)

    def readout(c):
        c4 = c.astype(F32).reshape(ns, SSM_SUB, SSM_GROUP, SSM_STATE)
        w = c4[:, :, :, None, :] * eye[None, :, None, :, None]
        return jnp.transpose(w, (0, 3, 4, 1, 2)).reshape(ns, SSM_SUB * SSM_STATE, SSM_SUB * SSM_GROUP)

    return dict(wbu=wbu.astype(BF16), lb_re=lb_re.reshape(1, -1), lb_im=lb_im.reshape(1, -1),
                wc_re=readout(c_re).astype(BF16), wc_im=readout(-c_im).astype(BF16))


def _ab_layer(x, w, s5w, norm_g, q_lora_g, kv_lora_g, d_skip, tabs, *, bsz, seq, past):
    m = x.shape[0]
    p = norm_matmul(x, norm_g, w["w_in"], tn=768)
    q = mla_q_prep(p, q_lora_g, w["w_uq"], tabs, w["gqn"], w["gqr"])
    lat, krp, kro = mla_latent(p, kv_lora_g, tabs)
    k, v = mla_kv_expand(lat, krp, w["wn"], w["wv"], w["gkn"], w["gkr"])
    p3 = p.reshape(bsz, seq, AB_COLS)
    gdim = s5w["lb_re"].shape[1]
    if past is None:
        a_out = mla_attn_prompt(q, k, v, p, bsz=bsz, seq=seq)
        h0 = jnp.zeros((bsz, gdim), F32)
        y, h_re, h_im = s5_mixer(p3, h0, h0, s5w["wbu"], s5w["lb_re"], s5w["lb_im"], s5w["wc_re"], s5w["wc_im"],
                                 d_skip.reshape(1, -1), tc=2 * CHUNK)
    else:
        lat_c, kr_c, h0_re, h0_im = past
        plen = lat_c.shape[1]
        kc, vc = mla_kv_expand(lat_c.reshape(bsz * plen, -1),
                               jnp.pad(kr_c.reshape(bsz * plen, -1), ((0, 0), (0, LANE - MLA_ROPE))),
                               w["wn"], w["wv"], w["gkn"], w["gkr"])
        a_out = mla_attn_step(q, kc, vc, k, v, p, bsz=bsz, t=seq, past=plen)
        y, h_re, h_im = s5_mixer(p3, h0_re.reshape(bsz, gdim), h0_im.reshape(bsz, gdim), s5w["wbu"], s5w["lb_re"],
                                 s5w["lb_im"], s5w["wc_re"], s5w["wc_im"], d_skip.reshape(1, -1), tc=seq)
    b_out = glu_gate(y.reshape(m, -1), w["wa"], w["wb"], p)
    x_new = out_proj([a_out, b_out], w["w_out"], x)
    return x_new, (lat, kro, h_re, h_im)


def _c_layer(x, w_in, w_out, norm_g, gq, gk, sinks, biases, *, bsz, seq, past):
    pc = norm_matmul(x, norm_g, w_in, tn=768)
    kn = swa_k_norm(pc, gk)
    v = pc[:, C_V:C_V + KVW]
    if past is None:
        a = swa_attn_prompt(pc, kn, gq, sinks, *biases, bsz=bsz, seq=seq)
        k_buf = kn.reshape(bsz, seq, KVW)[:, -WINDOW:]
        v_buf = v.reshape(bsz, seq, KVW)[:, -WINDOW:]
    else:
        k_c = past[0].reshape(bsz * WINDOW, KVW)
        v_c = past[1].reshape(bsz * WINDOW, KVW)
        a = swa_attn_step(pc, k_c, kn, v_c, gq, sinks, *biases, bsz=bsz, t=seq)
        k_buf = jnp.concatenate([k_c.reshape(bsz, WINDOW, KVW), kn.reshape(bsz, seq, KVW)], axis=1)[:, -WINDOW:]
        v_buf = jnp.concatenate([v_c.reshape(bsz, WINDOW, KVW), v.reshape(bsz, seq, KVW)], axis=1)[:, -WINDOW:]
    x_new = out_proj([a], w_out, x)
    shp = (bsz, WINDOW, SWA_KV_HEADS, SWA_HEAD_DIM)
    return x_new, (k_buf.reshape(shp), v_buf.reshape(shp))


def kernel(x_prompt, x_sample, cache_mla_latent, cache_mla_krope, state_ssm_re, state_ssm_im, cache_swa_k, cache_swa_v, rel_bias_table, ab_norm, ab_w_in, ab_q_lora_norm, ab_kv_lora_norm, ab_w_uq, ab_w_ukv, ab_q_norm, ab_k_norm, ssm_lambda_re, ssm_lambda_im, ssm_b_re, ssm_b_im, ssm_c_re, ssm_c_im, ssm_log_dt, ssm_d, ssm_w_glu, ab_w_out, c_norm, c_w_in, c_q_norm, c_k_norm, c_sinks, c_w_out):
    bp, lp, d = x_prompt.shape
    bs, ts, _ = x_sample.shape
    past_len = cache_mla_latent.shape[2]
    depth = ab_norm.shape[0] + c_norm.shape[0]
    assert lp % 256 == 0 and (bs * ts) % 128 == 0 and past_len % 512 == 0 and d == SWA_HEADS * SWA_HEAD_DIM

    q_pos = past_len + np.arange(ts)
    assert np.all((np.arange(past_len + ts)[None, :] // CHUNK) <= (q_pos[:, None] // CHUNK))
    k_pos = past_len - WINDOW + np.arange(WINDOW + ts)
    qch, kch = q_pos[:, None] // CHUNK, k_pos[None, :] // CHUNK
    assert np.all((kch <= qch) & (kch >= qch - WINDOW // CHUNK))

    tabs_p = _rope_tables(jnp.arange(lp))
    tabs_s = _rope_tables(past_len + jnp.arange(ts), reps=bs)

    qi = np.arange(WINDOW)[:, None]
    kj = np.arange(2 * WINDOW)[None, :]
    band = (kj // CHUNK - 2 <= qi // CHUNK) & (kj // CHUNK >= qi // CHUNK)
    bias_p = _rel_bias(rel_bias_table, jnp.asarray(kj - WINDOW - qi), jnp.asarray(band))
    biases_p = (jnp.stack([jnp.where(jnp.asarray(kj < WINDOW)[None], NEG, bias_p), bias_p]),)
    kj_s = np.arange(2 * WINDOW)[None, :]
    rel_s = np.where(kj_s < WINDOW + ts, (past_len - WINDOW + kj_s) - q_pos[:, None], 0)
    biases_s = (_rel_bias(rel_bias_table, jnp.asarray(rel_s),
                          jnp.asarray(np.broadcast_to(kj_s < WINDOW + ts, rel_s.shape))),)

    xp = x_prompt.reshape(bp * lp, d)
    xs = x_sample.reshape(bs * ts, d)
    outs_p = [[] for _ in range(6)]
    outs_s = [[] for _ in range(6)]
    for layer in range(depth):
        i = layer // 2
        if layer % 2 == 0:
            w = _prep_ab_weights(ab_w_in[i], ab_w_uq[i], ab_w_ukv[i], ab_q_norm[i], ab_k_norm[i],
                                 ssm_w_glu[i], ab_w_out[i])
            s5w = _prep_s5(ssm_lambda_re[i], ssm_lambda_im[i], ssm_log_dt[i], ssm_b_re[i], ssm_b_im[i],
                           ssm_c_re[i], ssm_c_im[i])
            args = (w, s5w, ab_norm[i], ab_q_lora_norm[i], ab_kv_lora_norm[i], ssm_d[i])
            xp, (a, b, c, e) = _ab_layer(xp, *args, tabs_p, bsz=bp, seq=lp, past=None)
            outs_p[0].append(a.reshape(bp, lp, -1)); outs_p[1].append(b.reshape(bp, lp, -1))
            outs_p[2].append(c.reshape(bp, -1, SSM_STATE)); outs_p[3].append(e.reshape(bp, -1, SSM_STATE))
            past = (cache_mla_latent[i], cache_mla_krope[i], state_ssm_re[i], state_ssm_im[i])
            xs, (a, b, c, e) = _ab_layer(xs, *args, tabs_s, bsz=bs, seq=ts, past=past)
            outs_s[0].append(a.reshape(bs, ts, -1)); outs_s[1].append(b.reshape(bs, ts, -1))
            outs_s[2].append(c.reshape(bs, -1, SSM_STATE)); outs_s[3].append(e.reshape(bs, -1, SSM_STATE))
        else:
            cw = c_w_in[i].astype(BF16)
            w_in = jnp.concatenate([cw[:, :d], cw[:, d + 2 * KVW:], cw[:, d:d + 2 * KVW]], axis=1)
            w_out = c_w_out[i].astype(BF16)
            args = (w_in, w_out, c_norm[i], c_q_norm[i], c_k_norm[i], c_sinks[i])
            xp, (a, b) = _c_layer(xp, *args, biases_p, bsz=bp, seq=lp, past=None)
            outs_p[4].append(a); outs_p[5].append(b)
            xs, (a, b) = _c_layer(xs, *args, biases_s, bsz=bs, seq=ts, past=(cache_swa_k[i], cache_swa_v[i]))
            outs_s[4].append(a); outs_s[5].append(b)
    return (xp.reshape(bp, lp, d), xs.reshape(bs, ts, d),
            *[jnp.stack(o) for o in outs_p], *[jnp.stack(o) for o in outs_s])
```

```python
import functools
import math

import jax
import jax.numpy as jnp
import numpy as np
from jax import lax
from jax.experimental import pallas as pl
from jax.experimental.pallas import tpu as pltpu

F32 = jnp.float32
BF16 = jnp.bfloat16

EPS = 1e-6
CHUNK = 64
ROPE_THETA = 10000.0
MLA_HEADS = 16
MLA_NOPE = 128
MLA_ROPE = 64
MLA_QK = MLA_NOPE + MLA_ROPE
MLA_V = 128
MLA_PAD = 256
SSM_GROUP = 16
SSM_STATE = 64
SSM_SUB = 8
SWA_HEADS = 64
SWA_KV_HEADS = 8
SWA_GQ = SWA_HEADS // SWA_KV_HEADS
SWA_HEAD_DIM = 64
WINDOW = 128
REL_BUCKETS = 32
REL_MAX_DIST = 128
NEG = -1e30
LANE = 128
VMEM_LIMIT = 56 * 1024 * 1024

AB_CQ = 0
AB_KR = 768
AB_CKV = 1024
AB_GA = 1536
AB_U = 3584
AB_GB = 5632
AB_COLS = 7680


def _cparams(*sem):
    return pltpu.CompilerParams(dimension_semantics=sem, vmem_limit_bytes=VMEM_LIMIT)


def _rms(x, g):
    return x * lax.rsqrt(jnp.mean(x * x, axis=-1, keepdims=True) + EPS) * g


def _silu(x):
    return x * jax.nn.sigmoid(x)


def _gelu_tanh(x):
    return 0.5 * x * (1.0 + jnp.tanh(math.sqrt(2.0 / math.pi) * (x + 0.044715 * (x * x * x))))


def _pick(n, prefs):
    for p in prefs:
        if n % p == 0:
            return p
    raise ValueError(f"no tile for {n} in {prefs}")


def _norm_mm_kernel(x_ref, g_ref, w_ref, o_ref, h_ref):
    @pl.when(pl.program_id(1) == 0)
    def _():
        h_ref[...] = _rms(x_ref[...], g_ref[...]).astype(BF16)

    o_ref[...] = jnp.dot(h_ref[...], w_ref[...], preferred_element_type=F32).astype(o_ref.dtype)


def norm_matmul(x, g, w, layer, *, tn, out_dtype=F32):
    m, k = x.shape
    n = w.shape[2]
    tm = _pick(m, (512, 256, 128))
    return pl.pallas_call(
        _norm_mm_kernel, name="norm_mm",
        grid=(m // tm, n // tn),
        in_specs=[pl.BlockSpec((tm, k), lambda i, j: (i, 0)),
                  pl.BlockSpec((1, k), lambda i, j: (0, 0)),
                  pl.BlockSpec((None, k, tn), lambda i, j: (layer, 0, j))],
        out_specs=pl.BlockSpec((tm, tn), lambda i, j: (i, j)),
        out_shape=jax.ShapeDtypeStruct((m, n), out_dtype),
        scratch_shapes=[pltpu.VMEM((tm, k), BF16)],
        compiler_params=_cparams("parallel", "arbitrary"),
    )(x, g.reshape(1, k), w)


def _rms_cast_kernel(x_ref, g_ref, o_ref):
    o_ref[...] = _rms(x_ref[...], g_ref[...]).astype(o_ref.dtype)


def rms_cast(x, g):
    m, k = x.shape
    tm = _pick(m, (512, 256, 128))
    return pl.pallas_call(
        _rms_cast_kernel, name="rms_cast",
        grid=(m // tm,),
        in_specs=[pl.BlockSpec((tm, k), lambda i: (i, 0)), pl.BlockSpec((1, k), lambda i: (0, 0))],
        out_specs=pl.BlockSpec((tm, k), lambda i: (i, 0)),
        out_shape=jax.ShapeDtypeStruct((m, k), BF16),
        compiler_params=_cparams("parallel"),
    )(x, g.reshape(1, k))


def _wcast_mm_kernel(h_ref, w_ref, o_ref, wb_ref):
    @pl.when(pl.program_id(1) == 0)
    def _():
        wb_ref[...] = w_ref[...].astype(BF16)

    o_ref[...] = jnp.dot(h_ref[...], wb_ref[...], preferred_element_type=F32).astype(o_ref.dtype)


def wcast_matmul(h, w, layer, out_col, *, tn):
    m, k = h.shape
    n = w.shape[2]
    tm = _pick(m, (1024, 512, 256, 128))
    return pl.pallas_call(
        _wcast_mm_kernel, name="wcast_mm",
        grid=(n // tn, m // tm),
        in_specs=[pl.BlockSpec((tm, k), lambda j, i: (i, 0)),
                  pl.BlockSpec((None, k, tn), lambda j, i: (layer, 0, j))],
        out_specs=pl.BlockSpec((tm, tn), lambda j, i: (i, out_col(j))),
        out_shape=jax.ShapeDtypeStruct((m, n), F32),
        scratch_shapes=[pltpu.VMEM((k, tn), BF16)],
        compiler_params=_cparams("arbitrary", "arbitrary"),
    )(h, w)


def _out_proj_kernel(*refs, n_lhs):
    lhs = refs[:n_lhs]
    w_ref, x_ref, o_ref, wb_ref = refs[n_lhs:]

    @pl.when(pl.program_id(1) == 0)
    def _():
        wb_ref[...] = w_ref[...].astype(BF16)

    acc = x_ref[...]
    off = 0
    for a in lhs:
        kk = a.shape[1]
        acc = acc + jnp.dot(a[...], wb_ref[off:off + kk, :], preferred_element_type=F32)
        off += kk
    o_ref[...] = acc


def out_proj(lhs, w, layer, x, *, tn=512):
    m, n = x.shape
    k = w.shape[1]
    tm = _pick(m, (1024, 512, 256, 128))
    in_specs = [pl.BlockSpec((tm, a.shape[1]), lambda j, i: (i, 0)) for a in lhs]
    in_specs += [pl.BlockSpec((None, k, tn), lambda j, i: (layer, 0, j)),
                 pl.BlockSpec((tm, tn), lambda j, i: (i, j))]
    return pl.pallas_call(
        functools.partial(_out_proj_kernel, n_lhs=len(lhs)), name="out_proj",
        grid=(n // tn, m // tm),
        in_specs=in_specs,
        out_specs=pl.BlockSpec((tm, tn), lambda j, i: (i, j)),
        out_shape=jax.ShapeDtypeStruct((m, n), F32),
        scratch_shapes=[pltpu.VMEM((k, tn), BF16)],
        compiler_params=_cparams("arbitrary", "arbitrary"),
    )(*lhs, w, x)


def _rope_slab(r, c, s1, s2):
    return r * c + pltpu.roll(r, 96, 1) * s1 + pltpu.roll(r, 32, 1) * s2


def _q_prep_kernel(cq_ref, g_ref, w_ref, c_ref, s1_ref, s2_ref, gn_ref, gr_ref, q_ref):
    h = _rms(cq_ref[...], g_ref[...]).astype(BF16)
    c, s1, s2 = c_ref[...], s1_ref[...], s2_ref[...]
    scale = MLA_QK ** -0.5 * math.log2(math.e)
    gn = gn_ref[...] * scale
    gr = gr_ref[...] * scale
    for hd in range(MLA_HEADS):
        lo = hd * MLA_PAD
        nope = jnp.dot(h, w_ref[:, lo:lo + LANE], preferred_element_type=F32)
        rr = _rope_slab(jnp.dot(h, w_ref[:, lo + LANE:lo + MLA_PAD], preferred_element_type=F32), c, s1, s2)
        ss = jnp.sum(nope * nope, axis=-1, keepdims=True) + jnp.sum(rr * rr, axis=-1, keepdims=True)
        rs = lax.rsqrt(ss * (1.0 / MLA_QK) + EPS)
        q_ref[:, lo:lo + LANE] = (nope * rs * gn).astype(BF16)
        q_ref[:, lo + LANE:lo + MLA_PAD] = (rr * rs * gr).astype(BF16)


def mla_q_prep(p, g, w_uq, layer, tabs, gn, gr):
    m = p.shape[0]
    period = tabs[0].shape[0]
    tm = _pick(period, (256, 128))
    nb = period // tm
    tab_spec = pl.BlockSpec((tm, LANE), lambda i: (i % nb, 0))
    vec = lambda n: pl.BlockSpec((1, n), lambda i: (0, 0))
    return pl.pallas_call(
        _q_prep_kernel, name="mla_q_prep",
        grid=(m // tm,),
        in_specs=[pl.BlockSpec((tm, 768), lambda i: (i, 0)), vec(768),
                  pl.BlockSpec((None,) + w_uq.shape[1:], lambda i: (layer, 0, 0)),
                  tab_spec, tab_spec, tab_spec, vec(LANE), vec(LANE)],
        out_specs=pl.BlockSpec((tm, MLA_HEADS * MLA_PAD), lambda i: (i, 0)),
        out_shape=jax.ShapeDtypeStruct((m, MLA_HEADS * MLA_PAD), BF16),
        compiler_params=_cparams("parallel"),
    )(p, g.reshape(1, -1), w_uq, *tabs, gn, gr)


def _latent_kernel(ckv_ref, kr_ref, g_ref, c_ref, s1_ref, s2_ref, lat_ref, krp_ref, kro_ref):
    lat_ref[...] = _rms(ckv_ref[...], g_ref[...])
    rr = _rope_slab(kr_ref[...], c_ref[...], s1_ref[...], s2_ref[...])
    krp_ref[...] = rr
    kro_ref[...] = rr[:, :MLA_ROPE]


def mla_latent(p, g, tabs):
    m = p.shape[0]
    period = tabs[0].shape[0]
    tm = _pick(period, (512, 256, 128))
    nb = period // tm
    tab_spec = pl.BlockSpec((tm, LANE), lambda i: (i % nb, 0))
    return pl.pallas_call(
        _latent_kernel, name="mla_latent",
        grid=(m // tm,),
        in_specs=[pl.BlockSpec((tm, 512), lambda i: (i, AB_CKV // 512)),
                  pl.BlockSpec((tm, LANE), lambda i: (i, AB_KR // LANE)),
                  pl.BlockSpec((1, 512), lambda i: (0, 0)),
                  tab_spec, tab_spec, tab_spec],
        out_specs=[pl.BlockSpec((tm, 512), lambda i: (i, 0)),
                   pl.BlockSpec((tm, LANE), lambda i: (i, 0)),
                   pl.BlockSpec((tm, MLA_ROPE), lambda i: (i, 0))],
        out_shape=[jax.ShapeDtypeStruct((m, 512), F32),
                   jax.ShapeDtypeStruct((m, LANE), F32),
                   jax.ShapeDtypeStruct((m, MLA_ROPE), F32)],
        compiler_params=_cparams("parallel"),
    )(p, p, g.reshape(1, -1), *tabs)


def _kv_expand_kernel(lat_ref, krp_ref, wn_ref, wv_ref, gn_ref, gr_ref, k_ref, v_ref):
    lat = lat_ref[...].astype(BF16)
    kr = krp_ref[...]
    ss_r = jnp.sum(kr * kr, axis=-1, keepdims=True)
    gn, gr = gn_ref[...], gr_ref[...]
    for hd in range(MLA_HEADS):
        kn = jnp.dot(lat, wn_ref[:, hd * LANE:(hd + 1) * LANE], preferred_element_type=F32)
        rs = lax.rsqrt((jnp.sum(kn * kn, axis=-1, keepdims=True) + ss_r) * (1.0 / MLA_QK) + EPS)
        lo = hd * MLA_PAD
        k_ref[:, lo:lo + LANE] = (kn * rs * gn).astype(BF16)
        k_ref[:, lo + LANE:lo + MLA_PAD] = (kr * rs * gr).astype(BF16)
        v_ref[:, hd * LANE:(hd + 1) * LANE] = jnp.dot(
            lat, wv_ref[:, hd * LANE:(hd + 1) * LANE], preferred_element_type=F32).astype(BF16)


def mla_kv_expand(lat, krp, src_layer, wn, wv, layer, gn, gr):
    m = lat.shape[1]
    tm = _pick(m, (512, 256, 128))
    vec = pl.BlockSpec((1, LANE), lambda i: (0, 0))
    return pl.pallas_call(
        _kv_expand_kernel, name="mla_kv_expand",
        grid=(m // tm,),
        in_specs=[pl.BlockSpec((None, tm, 512), lambda i: (src_layer, i, 0)),
                  pl.BlockSpec((None, tm, LANE), lambda i: (src_layer, i, 0)),
                  pl.BlockSpec((None,) + wn.shape[1:], lambda i: (layer, 0, 0)),
                  pl.BlockSpec((None,) + wv.shape[1:], lambda i: (layer, 0, 0)), vec, vec],
        out_specs=[pl.BlockSpec((tm, MLA_HEADS * MLA_PAD), lambda i: (i, 0)),
                   pl.BlockSpec((tm, MLA_HEADS * MLA_V), lambda i: (i, 0))],
        out_shape=[jax.ShapeDtypeStruct((m, MLA_HEADS * MLA_PAD), BF16),
                   jax.ShapeDtypeStruct((m, MLA_HEADS * MLA_V), BF16)],
        compiler_params=_cparams("parallel"),
    )(lat, krp, wn, wv, gn, gr)


def _softmax_step(q, kblk, vblk, carry, mask=None):
    m, l, acc = carry
    s = lax.dot_general(q, kblk, (((1,), (1,)), ((), ())), preferred_element_type=F32)
    if mask is not None:
        s = jnp.where(mask, s, NEG)
    m_new = jnp.maximum(m, jnp.max(s, axis=-1, keepdims=True))
    p = jnp.exp2(s - m_new)
    alpha = jnp.exp2(m - m_new)
    l = alpha * l + jnp.sum(p, axis=-1, keepdims=True)
    acc = alpha * acc + jnp.dot(p.astype(BF16), vblk, preferred_element_type=F32)
    return m_new, l, acc


def _mla_attn_kernel(q_ref, k_ref, v_ref, g_ref, o_ref, *, tq):
    i = pl.program_id(2)
    q = q_ref[...]

    def body(j, carry):
        off = pl.multiple_of(j * tq, tq)
        return _softmax_step(q, k_ref[pl.ds(off, tq), :], v_ref[pl.ds(off, tq), :], carry)

    init = (jnp.full((tq, 1), NEG, F32), jnp.zeros((tq, 1), F32), jnp.zeros((tq, MLA_V), F32))
    carry = lax.fori_loop(0, i, body, init)
    off = pl.multiple_of(i * tq, tq)
    row = lax.broadcasted_iota(jnp.int32, (tq, tq), 0) // CHUNK
    col = lax.broadcasted_iota(jnp.int32, (tq, tq), 1) // CHUNK
    _, l, acc = _softmax_step(q, k_ref[pl.ds(off, tq), :], v_ref[pl.ds(off, tq), :], carry, col <= row)
    o_ref[...] = (acc / l * _silu(g_ref[...])).astype(o_ref.dtype)


def mla_attn_prompt(q, k, v, p, *, bsz, seq):
    tq = _pick(seq, (512, 256))
    nq = seq // tq
    return pl.pallas_call(
        functools.partial(_mla_attn_kernel, tq=tq), name="mla_attn",
        grid=(bsz, MLA_HEADS, nq),
        in_specs=[pl.BlockSpec((tq, MLA_PAD), lambda b, h, i: (b * nq + i, h)),
                  pl.BlockSpec((seq, MLA_PAD), lambda b, h, i: (b, h)),
                  pl.BlockSpec((seq, MLA_V), lambda b, h, i: (b, h)),
                  pl.BlockSpec((tq, MLA_V), lambda b, h, i: (b * nq + i, AB_GA // MLA_V + h))],
        out_specs=pl.BlockSpec((tq, MLA_V), lambda b, h, i: (b * nq + i, h)),
        out_shape=jax.ShapeDtypeStruct((bsz * seq, MLA_HEADS * MLA_V), BF16),
        compiler_params=_cparams("parallel", "parallel", "arbitrary"),
    )(q, k, v, p)


def _mla_attn_step_kernel(q_ref, kc_ref, vc_ref, kn_ref, vn_ref, g_ref, o_ref, *, heads):
    t = q_ref.shape[0]
    init = (jnp.full((t, 1), NEG, F32), jnp.zeros((t, 1), F32), jnp.zeros((t, MLA_V), F32))
    for hd in range(heads):
        ks = slice(hd * MLA_PAD, (hd + 1) * MLA_PAD)
        vs = slice(hd * MLA_V, (hd + 1) * MLA_V)
        q = q_ref[:, ks]
        carry = _softmax_step(q, kc_ref[:, ks], vc_ref[:, vs], init)
        _, l, acc = _softmax_step(q, kn_ref[:, ks], vn_ref[:, vs], carry)
        o_ref[:, vs] = (acc / l * _silu(g_ref[:, vs])).astype(o_ref.dtype)


def mla_attn_step(q, kc, vc, kn, vn, p, *, bsz, t, past):
    heads = 4
    kw, vw = heads * MLA_PAD, heads * MLA_V
    return pl.pallas_call(
        functools.partial(_mla_attn_step_kernel, heads=heads), name="mla_attn_step",
        grid=(bsz, MLA_HEADS // heads),
        in_specs=[pl.BlockSpec((t, kw), lambda b, h: (b, h)),
                  pl.BlockSpec((past, kw), lambda b, h: (b, h)),
                  pl.BlockSpec((past, vw), lambda b, h: (b, h)),
                  pl.BlockSpec((t, kw), lambda b, h: (b, h)),
                  pl.BlockSpec((t, vw), lambda b, h: (b, h)),
                  pl.BlockSpec((t, vw), lambda b, h: (b, AB_GA // vw + h))],
        out_specs=pl.BlockSpec((t, vw), lambda b, h: (b, h)),
        out_shape=jax.ShapeDtypeStruct((bsz * t, MLA_HEADS * MLA_V), BF16),
        compiler_params=_cparams("parallel", "parallel"),
    )(q, kc, vc, kn, vn, p)


def _s5_disc_kernel(lre_ref, lim_ref, ldt_ref, bre_ref, bim_ref, lbre_ref, lbim_ref, bbre_ref, bbim_ref):
    lam_re = jnp.minimum(lre_ref[...], -1e-4)
    lam_im = lim_ref[...]
    dt = jnp.exp(ldt_ref[...])
    mag = jnp.exp(lam_re * dt)
    lb_re = mag * jnp.cos(lam_im * dt)
    lb_im = mag * jnp.sin(lam_im * dt)
    nr, ni = lb_re - 1.0, lb_im
    den = lam_re * lam_re + lam_im * lam_im
    f_re = (nr * lam_re + ni * lam_im) / den
    f_im = (ni * lam_re - nr * lam_im) / den
    b_re, b_im = bre_ref[...], bim_ref[...]
    lbre_ref[...] = lb_re
    lbim_ref[...] = lb_im
    bbre_ref[...] = f_re * b_re - f_im * b_im
    bbim_ref[...] = f_re * b_im + f_im * b_re


def s5_discretize(lam_re, lam_im, log_dt, b_re, b_im):
    g, pdim = lam_re.shape
    bt_re = jnp.swapaxes(b_re, 1, 2)
    bt_im = jnp.swapaxes(b_im, 1, 2)
    s3 = jax.ShapeDtypeStruct((g, 1, pdim), F32)
    sb = jax.ShapeDtypeStruct(bt_re.shape, F32)
    return pl.pallas_call(
        _s5_disc_kernel, name="s5_disc", out_shape=[s3, s3, sb, sb],
    )(lam_re.reshape(g, 1, pdim), lam_im.reshape(g, 1, pdim), log_dt.reshape(g, 1, 1), bt_re, bt_im)


def _s5_kernel(u_ref, h0re_ref, h0im_ref, wbu_ref, lbre_ref, lbim_ref, wcre_ref, wcim_ref, d_ref,
               y_ref, hre_ref, him_ref, ut_ref, sre_ref, sim_ref, *, bsz, tc, nsub, group, unroll):
    sub = 8
    assert bsz in (4, 8) and SSM_SUB * SSM_GROUP == LANE
    rows = bsz * tc
    sw = SSM_SUB * SSM_STATE
    uw = SSM_SUB * SSM_GROUP
    tiles = sw // LANE

    @pl.when(pl.program_id(1) == 0)
    def _():
        hre_ref[...] = h0re_ref[...]
        him_ref[...] = h0im_ref[...]

    for b in range(bsz):
        for s in range(nsub):
            ut_ref[s, pl.ds(b, tc, stride=bsz), :] = u_ref[b, :, s * uw:(s + 1) * uw]

    for s in range(nsub):
        bu = jnp.dot(ut_ref[s].astype(BF16), wbu_ref[s], preferred_element_type=F32)
        for k in range(tiles):
            sre_ref[s * tiles + k] = bu[:, k * LANE:(k + 1) * LANE]
            sim_ref[s * tiles + k] = bu[:, sw + k * LANE:sw + (k + 1) * LANE]

    low = lax.broadcasted_iota(jnp.int32, (sub, LANE), 0) < bsz
    for c0 in range(0, nsub * tiles, group):
        lanes = [slice((c0 + k) * LANE, (c0 + k + 1) * LANE) for k in range(group)]
        ar = [jnp.broadcast_to(lbre_ref[:, ls], (sub, LANE)) for ls in lanes]
        ai = [jnp.broadcast_to(lbim_ref[:, ls], (sub, LANE)) for ls in lanes]

        def step(i, carry, c0=c0, ar=ar, ai=ai):
            rs = pl.ds(pl.multiple_of(i * sub, sub), sub)
            out = []
            for k in range(group):
                hr, hi = carry[2 * k], carry[2 * k + 1]
                xr, xi = sre_ref[c0 + k, rs, :], sim_ref[c0 + k, rs, :]
                if bsz == sub:
                    nr = ar[k] * hr - ai[k] * hi + xr
                    ni = ar[k] * hi + ai[k] * hr + xi
                    sr, si = nr, ni
                else:
                    pr, pi = pltpu.roll(hr, bsz, 0), pltpu.roll(hi, bsz, 0)
                    v1r = ar[k] * pr - ai[k] * pi + xr
                    v1i = ar[k] * pi + ai[k] * pr + xi
                    qr, qi = pltpu.roll(v1r, bsz, 0), pltpu.roll(v1i, bsz, 0)
                    nr = ar[k] * qr - ai[k] * qi + xr
                    ni = ar[k] * qi + ai[k] * qr + xi
                    sr, si = jnp.where(low, v1r, nr), jnp.where(low, v1i, ni)
                sre_ref[c0 + k, rs, :] = sr
                sim_ref[c0 + k, rs, :] = si
                out += [nr, ni]
            return tuple(out)

        def widen(x):
            return x if bsz == sub else jnp.concatenate([x, x], axis=0)

        init = []
        for ls in lanes:
            init += [widen(hre_ref[:, ls]), widen(him_ref[:, ls])]
        fin = lax.fori_loop(0, rows // sub, step, tuple(init), unroll=unroll)
        for k, ls in enumerate(lanes):
            hre_ref[:, ls] = fin[2 * k][sub - bsz:]
            him_ref[:, ls] = fin[2 * k + 1][sub - bsz:]

    d = d_ref[...]
    for s in range(nsub):
        hre = jnp.concatenate([sre_ref[s * tiles + k] for k in range(tiles)], axis=-1).astype(BF16)
        him = jnp.concatenate([sim_ref[s * tiles + k] for k in range(tiles)], axis=-1).astype(BF16)
        y = jnp.dot(hre, wcre_ref[s], preferred_element_type=F32)
        y = y + jnp.dot(him, wcim_ref[s], preferred_element_type=F32)
        y = y + d[:, s * uw:(s + 1) * uw] * ut_ref[s]
        ut_ref[s] = _gelu_tanh(y)
    for b in range(bsz):
        for s in range(nsub):
            y_ref[b, :, s * uw:(s + 1) * uw] = ut_ref[s, pl.ds(b, tc, stride=bsz), :].astype(BF16)


def s5_mixer(p3, h0_re, h0_im, wbu, lb_re, lb_im, wc_re, wc_im, d_skip, *, tc):
    bsz, t, _ = p3.shape
    nsub = 4
    uw = nsub * SSM_SUB * SSM_GROUP
    sw = nsub * SSM_SUB * SSM_STATE
    nblk = wbu.shape[0] // nsub
    bw = uw * nblk
    kern = functools.partial(_s5_kernel, bsz=bsz, tc=tc, nsub=nsub, group=4, unroll=4)
    st_spec = pl.BlockSpec((bsz, sw), lambda g, c: (0, g))
    return pl.pallas_call(
        kern, name="s5_mixer",
        grid=(nblk, t // tc),
        in_specs=[pl.BlockSpec((bsz, tc, uw), lambda g, c: (0, c, AB_U // uw + g)),
                  st_spec, st_spec,
                  pl.BlockSpec((nsub,) + wbu.shape[1:], lambda g, c: (g, 0, 0)),
                  pl.BlockSpec((1, sw), lambda g, c: (0, g)),
                  pl.BlockSpec((1, sw), lambda g, c: (0, g)),
                  pl.BlockSpec((nsub,) + wc_re.shape[1:], lambda g, c: (g, 0, 0)),
                  pl.BlockSpec((nsub,) + wc_im.shape[1:], lambda g, c: (g, 0, 0)),
                  pl.BlockSpec((1, uw), lambda g, c: (0, g))],
        out_specs=[pl.BlockSpec((bsz, tc, uw), lambda g, c: (0, c, g)), st_spec, st_spec],
        out_shape=[jax.ShapeDtypeStruct((bsz, t, bw), BF16),
                   jax.ShapeDtypeStruct(h0_re.shape, F32),
                   jax.ShapeDtypeStruct(h0_im.shape, F32)],
        scratch_shapes=[pltpu.VMEM((nsub, bsz * tc, LANE), F32),
                        pltpu.VMEM((sw // LANE, bsz * tc, LANE), F32),
                        pltpu.VMEM((sw // LANE, bsz * tc, LANE), F32)],
        compiler_params=_cparams("parallel", "arbitrary"),
    )(p3, h0_re, h0_im, wbu, lb_re, lb_im, wc_re, wc_im, d_skip)


def _glu_kernel(y_ref, wa32_ref, wb32_ref, g_ref, o_ref, wa_ref, wb_ref):
    @pl.when(pl.program_id(1) == 0)
    def _():
        wa_ref[...] = wa32_ref[...].astype(BF16)
        wb_ref[...] = wb32_ref[...].astype(BF16)

    y = y_ref[...]
    ga = jnp.dot(y, wa_ref[...], preferred_element_type=F32)
    gb = jnp.dot(y, wb_ref[...], preferred_element_type=F32)
    o_ref[...] = (ga * jax.nn.sigmoid(gb) * _silu(g_ref[...])).astype(o_ref.dtype)


def glu_gate(y, w_glu, layer, p, *, tn=512):
    m, k = y.shape
    n = w_glu.shape[2] // 2
    tm = _pick(m, (1024, 512, 256, 128))
    nj = n // tn
    return pl.pallas_call(
        _glu_kernel, name="glu_gate",
        grid=(nj, m // tm),
        in_specs=[pl.BlockSpec((tm, k), lambda j, i: (i, 0)),
                  pl.BlockSpec((None, k, tn), lambda j, i: (layer, 0, j)),
                  pl.BlockSpec((None, k, tn), lambda j, i: (layer, 0, nj + j)),
                  pl.BlockSpec((tm, tn), lambda j, i: (i, AB_GB // tn + j))],
        out_specs=pl.BlockSpec((tm, tn), lambda j, i: (i, j)),
        out_shape=jax.ShapeDtypeStruct((m, n), BF16),
        scratch_shapes=[pltpu.VMEM((k, tn), BF16), pltpu.VMEM((k, tn), BF16)],
        compiler_params=_cparams("arbitrary", "arbitrary"),
    )(y, w_glu, w_glu, p)


C_Q = 0
C_G = 4096
C_K = 8192
C_V = 8704
KVW = SWA_KV_HEADS * SWA_HEAD_DIM


def _k_norm_kernel(k_ref, g_ref, o_ref):
    k = k_ref[...]
    g = g_ref[...]
    for h in range(SWA_KV_HEADS):
        sl = slice(h * SWA_HEAD_DIM, (h + 1) * SWA_HEAD_DIM)
        o_ref[:, sl] = _rms(k[:, sl], g)


def swa_k_norm(pc, gk):
    m = pc.shape[0]
    tm = _pick(m, (512, 256, 128))
    return pl.pallas_call(
        _k_norm_kernel, name="swa_k_norm",
        grid=(m // tm,),
        in_specs=[pl.BlockSpec((tm, KVW), lambda i: (i, C_K // KVW)),
                  pl.BlockSpec((1, SWA_HEAD_DIM), lambda i: (0, 0))],
        out_specs=pl.BlockSpec((tm, KVW), lambda i: (i, 0)),
        out_shape=jax.ShapeDtypeStruct((m, KVW), F32),
        compiler_params=_cparams("parallel"),
    )(pc, gk.reshape(1, -1))


def _swa_prompt_kernel(sink_ref, q_ref, kp_ref, kc_ref, vp_ref, vc_ref, g_ref, gq_ref, b_ref, o_ref, *, pair_batch):
    nk = b_ref.shape[-1]
    pad = nk - kp_ref.shape[0] - kc_ref.shape[0]
    fill = [jnp.zeros((pad, KVW), F32)] if pad else []
    kcat = jnp.concatenate([kp_ref[...], kc_ref[...]] + fill, axis=0)
    vcat = jnp.concatenate([vp_ref[...], vc_ref[...]] + fill, axis=0)
    seg_r = lax.broadcasted_iota(jnp.int32, (LANE, LANE), 0) // SWA_HEAD_DIM
    seg_c = lax.broadcasted_iota(jnp.int32, (LANE, LANE), 1) // SWA_HEAD_DIM
    seg_ones = (seg_r == seg_c).astype(BF16)
    left = lax.broadcasted_iota(jnp.int32, (1, LANE), 1) < SWA_HEAD_DIM
    gq = gq_ref[...] * (SWA_HEAD_DIM ** -0.5)
    nt = (((1,), (1,)), ((), ()))

    def halves(slab, half):
        own = jnp.where(left if half == 0 else jnp.logical_not(left), slab, 0.0)
        other = pltpu.roll(own, SWA_HEAD_DIM, 1)
        lo, hi = (own, other) if half == 0 else (other, own)
        return lo.astype(BF16), hi.astype(BF16)

    kv = {}

    def kv_halves(kvh):
        if kvh not in kv:
            ks = slice((kvh // 2) * LANE, (kvh // 2 + 1) * LANE)
            kv.clear()
            kv[kvh] = halves(kcat[:, ks], kvh % 2) + halves(vcat[:, ks], kvh % 2)
        return kv[kvh]

    def scores(pair):
        sl = slice(pair * LANE, (pair + 1) * LANE)
        k_lo, k_hi, _, _ = kv_halves(pair // (SWA_GQ // 2))
        q = q_ref[:, sl]
        x2 = q * q
        x2_hi = x2.astype(BF16)
        x2_lo = (x2 - x2_hi.astype(F32)).astype(BF16)
        ss = (jnp.dot(x2_hi, seg_ones, preferred_element_type=F32)
              + jnp.dot(x2_lo, seg_ones, preferred_element_type=F32))
        qn = (q * lax.rsqrt(ss * (1.0 / SWA_HEAD_DIM) + EPS) * gq).astype(BF16)
        return [lax.dot_general(qn, k_lo, nt, preferred_element_type=F32) + b_ref[2 * pair],
                lax.dot_general(qn, k_hi, nt, preferred_element_type=F32) + b_ref[2 * pair + 1]]

    def finish(pairs, ss, vhs):
        sinks = [sink_ref[2 * p + j] for p in pairs for j in range(2)]
        flat = [s for pr in ss for s in pr]
        ms = [jnp.maximum(jnp.max(s, axis=-1, keepdims=True), sk) for s, sk in zip(flat, sinks)]
        es = [jnp.exp(s - m) for s, m in zip(flat, ms)]
        ds = [jnp.sum(e, axis=-1, keepdims=True) + jnp.exp(sk - m) for e, m, sk in zip(es, ms, sinks)]
        for i, p in enumerate(pairs):
            v_lo, v_hi = vhs[i]
            o = (jnp.dot(es[2 * i].astype(BF16), v_lo, preferred_element_type=F32)
                 + jnp.dot(es[2 * i + 1].astype(BF16), v_hi, preferred_element_type=F32))
            inv = jnp.where(left, 1.0 / ds[2 * i], 1.0 / ds[2 * i + 1])
            sl = slice(p * LANE, (p + 1) * LANE)
            o_ref[:, sl] = (o * inv * _silu(g_ref[:, sl])).astype(o_ref.dtype)

    npairs = SWA_HEADS // 2
    batches = [list(range(b, b + pair_batch)) for b in range(0, npairs, pair_batch)]
    nxt = ([scores(p) for p in batches[0]], [kv_halves(p // (SWA_GQ // 2))[2:] for p in batches[0]])
    for bi, pairs in enumerate(batches):
        cur = nxt
        if bi + 1 < len(batches):
            nb = batches[bi + 1]
            nxt = ([scores(p) for p in nb], [kv_halves(p // (SWA_GQ // 2))[2:] for p in nb])
        finish(pairs, *cur)


def swa_attn_prompt(pc, kn, gq, sinks, bias, *, bsz, seq):
    tq = WINDOW
    nq = seq // tq
    dm = SWA_HEADS * SWA_HEAD_DIM
    prev = lambda b, i: (b * nq + jnp.maximum(i - 1, 0), 0)
    cur = lambda b, i: (b * nq + i, 0)
    return pl.pallas_call(
        functools.partial(_swa_prompt_kernel, pair_batch=2), name="swa_attn",
        grid=(bsz, nq),
        in_specs=[pl.BlockSpec(memory_space=pltpu.SMEM),
                  pl.BlockSpec((tq, dm), lambda b, i: (b * nq + i, C_Q // dm)),
                  pl.BlockSpec((tq, KVW), prev), pl.BlockSpec((tq, KVW), cur),
                  pl.BlockSpec((tq, KVW), lambda b, i: (b * nq + jnp.maximum(i - 1, 0), C_V // KVW)),
                  pl.BlockSpec((tq, KVW), lambda b, i: (b * nq + i, C_V // KVW)),
                  pl.BlockSpec((tq, dm), lambda b, i: (b * nq + i, C_G // dm)),
                  pl.BlockSpec((1, LANE), lambda b, i: (0, 0)),
                  pl.BlockSpec((None,) + bias.shape[1:], lambda b, i: (jnp.minimum(i, 1), 0, 0, 0))],
        out_specs=pl.BlockSpec((tq, dm), cur),
        out_shape=jax.ShapeDtypeStruct((bsz * seq, dm), BF16),
        compiler_params=_cparams("parallel", "arbitrary"),
    )(sinks, pc, kn, kn, pc, pc, pc, jnp.tile(gq.reshape(1, -1), (1, 2)), bias)


def swa_attn_step(pc, k_cache, kn, v_cache, layer, gq, sinks, bias, *, bsz, t):
    dm = SWA_HEADS * SWA_HEAD_DIM
    row = lambda b: (b, 0)
    cache = pl.BlockSpec((None, WINDOW, KVW), lambda b: (layer, b, 0))
    return pl.pallas_call(
        functools.partial(_swa_prompt_kernel, pair_batch=4), name="swa_attn_step",
        grid=(bsz,),
        in_specs=[pl.BlockSpec(memory_space=pltpu.SMEM),
                  pl.BlockSpec((t, dm), lambda b: (b, C_Q // dm)),
                  cache, pl.BlockSpec((t, KVW), row), cache,
                  pl.BlockSpec((t, KVW), lambda b: (b, C_V // KVW)),
                  pl.BlockSpec((t, dm), lambda b: (b, C_G // dm)),
                  pl.BlockSpec((1, LANE), lambda b: (0, 0)),
                  pl.BlockSpec(bias.shape, lambda b: (0, 0, 0))],
        out_specs=pl.BlockSpec((t, dm), row),
        out_shape=jax.ShapeDtypeStruct((bsz * t, dm), BF16),
        compiler_params=_cparams("parallel"),
    )(sinks, pc, k_cache, kn, v_cache, pc, pc, jnp.tile(gq.reshape(1, -1), (1, 2)), bias)


def _rope_tables(pos, reps=1):
    half = MLA_ROPE // 2
    inv = ROPE_THETA ** (-jnp.arange(half, dtype=F32) / half)
    ang = pos.astype(F32)[:, None] * inv
    cos, sin = jnp.cos(ang), jnp.sin(ang)
    z = jnp.zeros_like(cos)
    tabs = (jnp.concatenate([cos, cos, z, z], -1), jnp.concatenate([-sin, z, z, z], -1),
            jnp.concatenate([z, sin, z, z], -1))
    return tuple(jnp.tile(t, (reps, 1)) for t in tabs)


def _rel_bucket(rel):
    nb = REL_BUCKETS // 2
    max_exact = nb // 2
    ret = jnp.where(rel > 0, nb, 0)
    n = jnp.abs(rel)
    nf = jnp.maximum(n, 1).astype(F32)
    large = max_exact + (jnp.log(nf / max_exact) / math.log(REL_MAX_DIST / max_exact)
                         * (nb - max_exact)).astype(jnp.int32)
    large = jnp.minimum(large, nb - 1)
    return ret + jnp.where(n < max_exact, n, large)


def _rel_bias(table, rel, mask):
    onehot = (_rel_bucket(rel)[:, :, None] == jnp.arange(REL_BUCKETS)).astype(F32)
    b = jnp.einsum("qkb,bh->hqk", onehot, table.astype(F32), precision=lax.Precision.HIGHEST)
    return jnp.where(mask[None], b, NEG)


def _prep_ab_weights(w_in, w_uq, w_ukv):
    nl, d, _ = w_in.shape
    wb = w_in.astype(BF16)
    w_in_r = jnp.concatenate([wb[:, :, 0:768], wb[:, :, 1280:1344],
                              jnp.zeros((nl, d, AB_CKV - AB_KR - MLA_ROPE), BF16),
                              wb[:, :, 768:1280], wb[:, :, 1344:]], axis=2)
    uq = w_uq.astype(BF16).reshape(nl, w_uq.shape[1], MLA_HEADS, MLA_QK)
    uq = jnp.pad(uq, ((0, 0), (0, 0), (0, 0), (0, MLA_PAD - MLA_QK))).reshape(nl, w_uq.shape[1], -1)
    ukv = w_ukv.astype(BF16).reshape(nl, w_ukv.shape[1], MLA_HEADS, MLA_NOPE + MLA_V)
    wn = ukv[:, :, :, :MLA_NOPE].reshape(nl, w_ukv.shape[1], -1)
    wv = ukv[:, :, :, MLA_NOPE:].reshape(nl, w_ukv.shape[1], -1)
    return dict(w_in=w_in_r, w_uq=uq, wn=wn, wv=wv)


def _head_gains(g):
    return g[:MLA_NOPE].reshape(1, LANE), jnp.pad(g[MLA_NOPE:], (0, LANE - MLA_ROPE)).reshape(1, LANE)


def _prep_s5(lam_re, lam_im, log_dt, b_re, b_im, c_re, c_im):
    g = lam_re.shape[0]
    lb_re, lb_im, bbt_re, bbt_im = s5_discretize(lam_re, lam_im, log_dt, b_re, b_im)
    eye = jnp.eye(SSM_SUB, dtype=F32)
    ns = g // SSM_SUB
    bb = jnp.stack([bbt_re, bbt_im]).reshape(2, ns, SSM_SUB, SSM_GROUP, SSM_STATE)
    wbu = bb[:, :, :, :, None, :] * eye[None, None, :, None, :, None]
    wbu = jnp.transpose(wbu, (1, 2, 3, 0, 4, 5)).reshape(ns, SSM_SUB * SSM_GROUP, 2 * SSM_SUB * SSM_STATE)

    def readout(c):
        c4 = c.astype(F32).reshape(ns, SSM_SUB, SSM_GROUP, SSM_STATE)
        w = c4[:, :, :, None, :] * eye[None, :, None, :, None]
        return jnp.transpose(w, (0, 3, 4, 1, 2)).reshape(ns, SSM_SUB * SSM_STATE, SSM_SUB * SSM_GROUP)

    return dict(wbu=wbu.astype(BF16), lb_re=lb_re.reshape(1, -1), lb_im=lb_im.reshape(1, -1),
                wc_re=readout(c_re).astype(BF16), wc_im=readout(-c_im).astype(BF16))


def _ab_layer(x, i, w, prm, s5w, tabs, *, bsz, seq, past):
    m = x.shape[0]
    gqn, gqr = _head_gains(prm["q_norm"][i])
    gkn, gkr = _head_gains(prm["k_norm"][i])
    p = norm_matmul(x, prm["norm"][i], w["w_in"], i, tn=768)
    q = mla_q_prep(p, prm["q_lora_norm"][i], w["w_uq"], i, tabs, gqn, gqr)
    lat, krp, kro = mla_latent(p, prm["kv_lora_norm"][i], tabs)
    k, v = mla_kv_expand(lat[None], krp[None], 0, w["wn"], w["wv"], i, gkn, gkr)
    p3 = p.reshape(bsz, seq, AB_COLS)
    gdim = s5w["lb_re"].shape[1]
    d_skip = prm["d_skip"][i].reshape(1, -1)
    s5args = (s5w["wbu"], s5w["lb_re"], s5w["lb_im"], s5w["wc_re"], s5w["wc_im"], d_skip)
    if past is None:
        a_out = mla_attn_prompt(q, k, v, p, bsz=bsz, seq=seq)
        h0 = jnp.zeros((bsz, gdim), F32)
        y, h_re, h_im = s5_mixer(p3, h0, h0, *s5args, tc=2 * CHUNK)
    else:
        lat_c, kr_c, h0_re, h0_im = past
        kc, vc = mla_kv_expand(lat_c, kr_c, i, w["wn"], w["wv"], i, gkn, gkr)
        a_out = mla_attn_step(q, kc, vc, k, v, p, bsz=bsz, t=seq, past=lat_c.shape[1] // bsz)
        y, h_re, h_im = s5_mixer(p3, h0_re.reshape(bsz, gdim), h0_im.reshape(bsz, gdim), *s5args, tc=seq)
    b_out = glu_gate(y.reshape(m, -1), prm["w_glu"], i, p)
    x_new = out_proj([a_out, b_out], prm["w_out"], i, x)
    return x_new, (lat, kro, h_re, h_im)


def _c_in_col(j):
    return jnp.where(j < 8, j, jnp.where(j < 10, j + 8, j - 2))


def _c_layer(x, i, prm, biases, *, bsz, seq, past):
    assert prm["w_in"].shape[2] == C_V + KVW and C_G == 8 * KVW and C_K == 16 * KVW
    pc = wcast_matmul(rms_cast(x, prm["norm"][i]), prm["w_in"], i, _c_in_col, tn=KVW)
    gq, gk, sinks = prm["q_norm"][i], prm["k_norm"][i], prm["sinks"][i]
    kn = swa_k_norm(pc, gk)
    v = pc[:, C_V:C_V + KVW]
    if past is None:
        a = swa_attn_prompt(pc, kn, gq, sinks, *biases, bsz=bsz, seq=seq)
        k_buf = kn.reshape(bsz, seq, KVW)[:, -WINDOW:]
        v_buf = v.reshape(bsz, seq, KVW)[:, -WINDOW:]
    else:
        k_c, v_c = past
        a = swa_attn_step(pc, k_c, kn, v_c, i, gq, sinks, *biases, bsz=bsz, t=seq)
        k_buf = jnp.concatenate([k_c[i].reshape(bsz, WINDOW, KVW), kn.reshape(bsz, seq, KVW)], axis=1)[:, -WINDOW:]
        v_buf = jnp.concatenate([v_c[i].reshape(bsz, WINDOW, KVW), v.reshape(bsz, seq, KVW)], axis=1)[:, -WINDOW:]
    x_new = out_proj([a], prm["w_out"], i, x)
    shp = (bsz, WINDOW, SWA_KV_HEADS, SWA_HEAD_DIM)
    return x_new, (k_buf.reshape(shp), v_buf.reshape(shp))


def kernel(x_prompt, x_sample, cache_mla_latent, cache_mla_krope, state_ssm_re, state_ssm_im, cache_swa_k, cache_swa_v, rel_bias_table, ab_norm, ab_w_in, ab_q_lora_norm, ab_kv_lora_norm, ab_w_uq, ab_w_ukv, ab_q_norm, ab_k_norm, ssm_lambda_re, ssm_lambda_im, ssm_b_re, ssm_b_im, ssm_c_re, ssm_c_im, ssm_log_dt, ssm_d, ssm_w_glu, ab_w_out, c_norm, c_w_in, c_q_norm, c_k_norm, c_sinks, c_w_out):
    bp, lp, d = x_prompt.shape
    bs, ts, _ = x_sample.shape
    past_len = cache_mla_latent.shape[2]
    depth = ab_norm.shape[0] + c_norm.shape[0]
    assert lp % 256 == 0 and (bs * ts) % 128 == 0 and past_len % 512 == 0 and d == SWA_HEADS * SWA_HEAD_DIM

    q_pos = past_len + np.arange(ts)
    assert np.all((np.arange(past_len + ts)[None, :] // CHUNK) <= (q_pos[:, None] // CHUNK))
    k_pos = past_len - WINDOW + np.arange(WINDOW + ts)
    qch, kch = q_pos[:, None] // CHUNK, k_pos[None, :] // CHUNK
    assert np.all((kch <= qch) & (kch >= qch - WINDOW // CHUNK))

    tabs_p = _rope_tables(jnp.arange(lp))
    tabs_s = _rope_tables(past_len + jnp.arange(ts), reps=bs)

    qi = np.arange(WINDOW)[:, None]
    kj = np.arange(2 * WINDOW)[None, :]
    band = (kj // CHUNK - 2 <= qi // CHUNK) & (kj // CHUNK >= qi // CHUNK)
    bias_p = _rel_bias(rel_bias_table, jnp.asarray(kj - WINDOW - qi), jnp.asarray(band))
    biases_p = (jnp.stack([jnp.where(jnp.asarray(kj < WINDOW)[None], NEG, bias_p), bias_p]),)
    kj_s = np.arange(2 * WINDOW)[None, :]
    rel_s = np.where(kj_s < WINDOW + ts, (past_len - WINDOW + kj_s) - q_pos[:, None], 0)
    biases_s = (_rel_bias(rel_bias_table, jnp.asarray(rel_s),
                          jnp.asarray(np.broadcast_to(kj_s < WINDOW + ts, rel_s.shape))),)

    xp = x_prompt.reshape(bp * lp, d)
    xs = x_sample.reshape(bs * ts, d)
    outs_p = [[] for _ in range(6)]
    outs_s = [[] for _ in range(6)]
    n_ab = ab_norm.shape[0]
    ab_w = _prep_ab_weights(ab_w_in, ab_w_uq, ab_w_ukv)
    ab_prm = dict(norm=ab_norm, q_lora_norm=ab_q_lora_norm, kv_lora_norm=ab_kv_lora_norm, q_norm=ab_q_norm,
                  k_norm=ab_k_norm, d_skip=ssm_d, w_glu=ssm_w_glu, w_out=ab_w_out)
    c_prm = dict(norm=c_norm, w_in=c_w_in, q_norm=c_q_norm, k_norm=c_k_norm, sinks=c_sinks, w_out=c_w_out)
    lat_cache = cache_mla_latent.reshape(n_ab, bs * past_len, -1)
    kr_cache = jnp.pad(cache_mla_krope.reshape(n_ab, bs * past_len, -1), ((0, 0), (0, 0), (0, LANE - MLA_ROPE)))
    swa_past = (cache_swa_k.reshape(-1, bs * WINDOW, KVW), cache_swa_v.reshape(-1, bs * WINDOW, KVW))
    for layer in range(depth):
        i = layer // 2
        if layer % 2 == 0:
            s5w = _prep_s5(ssm_lambda_re[i], ssm_lambda_im[i], ssm_log_dt[i], ssm_b_re[i], ssm_b_im[i],
                           ssm_c_re[i], ssm_c_im[i])
            xp, (a, b, c, e) = _ab_layer(xp, i, ab_w, ab_prm, s5w, tabs_p, bsz=bp, seq=lp, past=None)
            outs_p[0].append(a.reshape(bp, lp, -1)); outs_p[1].append(b.reshape(bp, lp, -1))
            outs_p[2].append(c.reshape(bp, -1, SSM_STATE)); outs_p[3].append(e.reshape(bp, -1, SSM_STATE))
            past = (lat_cache, kr_cache, state_ssm_re[i], state_ssm_im[i])
            xs, (a, b, c, e) = _ab_layer(xs, i, ab_w, ab_prm, s5w, tabs_s, bsz=bs, seq=ts, past=past)
            outs_s[0].append(a.reshape(bs, ts, -1)); outs_s[1].append(b.reshape(bs, ts, -1))
            outs_s[2].append(c.reshape(bs, -1, SSM_STATE)); outs_s[3].append(e.reshape(bs, -1, SSM_STATE))
        else:
            xp, (a, b) = _c_layer(xp, i, c_prm, biases_p, bsz=bp, seq=lp, past=None)
            outs_p[4].append(a); outs_p[5].append(b)
            xs, (a, b) = _c_layer(xs, i, c_prm, biases_s, bsz=bs, seq=ts, past=swa_past)
            outs_s[4].append(a); outs_s[5].append(b)
    return (xp.reshape(bp, lp, d), xs.reshape(bs, ts, d),
            *[jnp.stack(o) for o in outs_p], *[jnp.stack(o) for o in outs_s])
```

```python
import functools
import math

import jax
import jax.numpy as jnp
import numpy as np
from jax import lax
from jax.experimental import pallas as pl
from jax.experimental.pallas import tpu as pltpu

F32 = jnp.float32
BF16 = jnp.bfloat16

EPS = 1e-6
CHUNK = 64
ROPE_THETA = 10000.0
MLA_HEADS = 16
MLA_NOPE = 128
MLA_ROPE = 64
MLA_QK = MLA_NOPE + MLA_ROPE
MLA_V = 128
MLA_PAD = 256
SSM_GROUP = 16
SSM_STATE = 64
SSM_SUB = 8
SWA_HEADS = 64
SWA_KV_HEADS = 8
SWA_GQ = SWA_HEADS // SWA_KV_HEADS
SWA_HEAD_DIM = 64
WINDOW = 128
REL_BUCKETS = 32
REL_MAX_DIST = 128
NEG = -1e30
LOG2E = math.log2(math.e)
LANE = 128
VMEM_LIMIT = 56 * 1024 * 1024

AB_CQ = 0
AB_KR = 768
AB_CKV = 1024
AB_GA = 1536
AB_U = 3584
AB_GB = 5632
AB_COLS = 7680


def _cparams(*sem):
    return pltpu.CompilerParams(dimension_semantics=sem, vmem_limit_bytes=VMEM_LIMIT)


def _rms(x, g):
    return x * lax.rsqrt(jnp.mean(x * x, axis=-1, keepdims=True) + EPS) * g


def _silu(x):
    return x * jax.nn.sigmoid(x)


def _gelu_tanh(x):
    return 0.5 * x * (1.0 + jnp.tanh(math.sqrt(2.0 / math.pi) * (x + 0.044715 * (x * x * x))))


def _pick(n, prefs):
    for p in prefs:
        if n % p == 0:
            return p
    raise ValueError(f"no tile for {n} in {prefs}")


def _norm_mm_kernel(x_ref, g_ref, w_ref, o_ref, h_ref):
    @pl.when(pl.program_id(1) == 0)
    def _():
        h_ref[...] = _rms(x_ref[...], g_ref[...]).astype(BF16)

    o_ref[...] = jnp.dot(h_ref[...], w_ref[...], preferred_element_type=F32).astype(o_ref.dtype)


def norm_matmul(x, g, w, layer, *, tn, out_dtype=F32):
    m, k = x.shape
    n = w.shape[2]
    tm = _pick(m, (512, 256, 128))
    return pl.pallas_call(
        _norm_mm_kernel, name="norm_mm",
        grid=(m // tm, n // tn),
        in_specs=[pl.BlockSpec((tm, k), lambda i, j: (i, 0)),
                  pl.BlockSpec((1, k), lambda i, j: (0, 0)),
                  pl.BlockSpec((None, k, tn), lambda i, j: (layer, 0, j))],
        out_specs=pl.BlockSpec((tm, tn), lambda i, j: (i, j)),
        out_shape=jax.ShapeDtypeStruct((m, n), out_dtype),
        scratch_shapes=[pltpu.VMEM((tm, k), BF16)],
        compiler_params=_cparams("parallel", "arbitrary"),
    )(x, g.reshape(1, k), w)


def _rms_cast_kernel(x_ref, g_ref, o_ref):
    o_ref[...] = _rms(x_ref[...], g_ref[...]).astype(o_ref.dtype)


def rms_cast(x, g):
    m, k = x.shape
    tm = _pick(m, (512, 256, 128))
    return pl.pallas_call(
        _rms_cast_kernel, name="rms_cast",
        grid=(m // tm,),
        in_specs=[pl.BlockSpec((tm, k), lambda i: (i, 0)), pl.BlockSpec((1, k), lambda i: (0, 0))],
        out_specs=pl.BlockSpec((tm, k), lambda i: (i, 0)),
        out_shape=jax.ShapeDtypeStruct((m, k), BF16),
        compiler_params=_cparams("parallel"),
    )(x, g.reshape(1, k))


def _wcast_mm_kernel(h_ref, w_ref, o_ref, wb_ref):
    @pl.when(pl.program_id(1) == 0)
    def _():
        wb_ref[...] = w_ref[...].astype(BF16)

    o_ref[...] = jnp.dot(h_ref[...], wb_ref[...], preferred_element_type=F32).astype(o_ref.dtype)


def wcast_matmul(h, w, layer, out_col, *, tn):
    m, k = h.shape
    n = w.shape[2]
    tm = _pick(m, (1024, 512, 256, 128))
    return pl.pallas_call(
        _wcast_mm_kernel, name="wcast_mm",
        grid=(n // tn, m // tm),
        in_specs=[pl.BlockSpec((tm, k), lambda j, i: (i, 0)),
                  pl.BlockSpec((None, k, tn), lambda j, i: (layer, 0, j))],
        out_specs=pl.BlockSpec((tm, tn), lambda j, i: (i, out_col(j))),
        out_shape=jax.ShapeDtypeStruct((m, n), F32),
        scratch_shapes=[pltpu.VMEM((k, tn), BF16)],
        compiler_params=_cparams("arbitrary", "arbitrary"),
    )(h, w)


def _out_proj_kernel(*refs, n_lhs):
    lhs = refs[:n_lhs]
    w_ref, x_ref, o_ref, wb_ref = refs[n_lhs:]

    @pl.when(pl.program_id(1) == 0)
    def _():
        wb_ref[...] = w_ref[...].astype(BF16)

    acc = x_ref[...]
    off = 0
    for a in lhs:
        kk = a.shape[1]
        acc = acc + jnp.dot(a[...], wb_ref[off:off + kk, :], preferred_element_type=F32)
        off += kk
    o_ref[...] = acc


def out_proj(lhs, w, layer, x, *, tn=512):
    m, n = x.shape
    k = w.shape[1]
    tm = _pick(m, (1024, 512, 256, 128))
    in_specs = [pl.BlockSpec((tm, a.shape[1]), lambda j, i: (i, 0)) for a in lhs]
    in_specs += [pl.BlockSpec((None, k, tn), lambda j, i: (layer, 0, j)),
                 pl.BlockSpec((tm, tn), lambda j, i: (i, j))]
    return pl.pallas_call(
        functools.partial(_out_proj_kernel, n_lhs=len(lhs)), name="out_proj",
        grid=(n // tn, m // tm),
        in_specs=in_specs,
        out_specs=pl.BlockSpec((tm, tn), lambda j, i: (i, j)),
        out_shape=jax.ShapeDtypeStruct((m, n), F32),
        scratch_shapes=[pltpu.VMEM((k, tn), BF16)],
        compiler_params=_cparams("arbitrary", "arbitrary"),
    )(*lhs, w, x)


def _rope_slab(r, c, s1, s2):
    return r * c + pltpu.roll(r, 96, 1) * s1 + pltpu.roll(r, 32, 1) * s2


def _q_prep_kernel(cq_ref, g_ref, w_ref, c_ref, s1_ref, s2_ref, gn_ref, gr_ref, q_ref):
    h = _rms(cq_ref[...], g_ref[...]).astype(BF16)
    c, s1, s2 = c_ref[...], s1_ref[...], s2_ref[...]
    scale = MLA_QK ** -0.5 * math.log2(math.e)
    gn = gn_ref[...] * scale
    gr = gr_ref[...] * scale
    for hd in range(MLA_HEADS):
        lo = hd * MLA_PAD
        nope = jnp.dot(h, w_ref[:, lo:lo + LANE], preferred_element_type=F32)
        rr = _rope_slab(jnp.dot(h, w_ref[:, lo + LANE:lo + MLA_PAD], preferred_element_type=F32), c, s1, s2)
        ss = jnp.sum(nope * nope, axis=-1, keepdims=True) + jnp.sum(rr * rr, axis=-1, keepdims=True)
        rs = lax.rsqrt(ss * (1.0 / MLA_QK) + EPS)
        q_ref[:, lo:lo + LANE] = (nope * rs * gn).astype(BF16)
        q_ref[:, lo + LANE:lo + MLA_PAD] = (rr * rs * gr).astype(BF16)


def mla_q_prep(p, g, w_uq, layer, tabs, gn, gr):
    m = p.shape[0]
    period = tabs[0].shape[0]
    tm = _pick(period, (512, 256, 128))
    nb = period // tm
    tab_spec = pl.BlockSpec((tm, LANE), lambda i: (i % nb, 0))
    vec = lambda n: pl.BlockSpec((1, n), lambda i: (0, 0))
    return pl.pallas_call(
        _q_prep_kernel, name="mla_q_prep",
        grid=(m // tm,),
        in_specs=[pl.BlockSpec((tm, 768), lambda i: (i, 0)), vec(768),
                  pl.BlockSpec((None,) + w_uq.shape[1:], lambda i: (layer, 0, 0)),
                  tab_spec, tab_spec, tab_spec, vec(LANE), vec(LANE)],
        out_specs=pl.BlockSpec((tm, MLA_HEADS * MLA_PAD), lambda i: (i, 0)),
        out_shape=jax.ShapeDtypeStruct((m, MLA_HEADS * MLA_PAD), BF16),
        compiler_params=_cparams("parallel"),
    )(p, g.reshape(1, -1), w_uq, *tabs, gn, gr)


def _latent_kernel(ckv_ref, kr_ref, g_ref, c_ref, s1_ref, s2_ref, lat_ref, krp_ref, kro_ref):
    lat_ref[...] = _rms(ckv_ref[...], g_ref[...])
    rr = _rope_slab(kr_ref[...], c_ref[...], s1_ref[...], s2_ref[...])
    krp_ref[...] = rr
    kro_ref[...] = rr[:, :MLA_ROPE]


def mla_latent(p, g, tabs):
    m = p.shape[0]
    period = tabs[0].shape[0]
    tm = _pick(period, (512, 256, 128))
    nb = period // tm
    tab_spec = pl.BlockSpec((tm, LANE), lambda i: (i % nb, 0))
    return pl.pallas_call(
        _latent_kernel, name="mla_latent",
        grid=(m // tm,),
        in_specs=[pl.BlockSpec((tm, 512), lambda i: (i, AB_CKV // 512)),
                  pl.BlockSpec((tm, LANE), lambda i: (i, AB_KR // LANE)),
                  pl.BlockSpec((1, 512), lambda i: (0, 0)),
                  tab_spec, tab_spec, tab_spec],
        out_specs=[pl.BlockSpec((tm, 512), lambda i: (i, 0)),
                   pl.BlockSpec((tm, LANE), lambda i: (i, 0)),
                   pl.BlockSpec((tm, MLA_ROPE), lambda i: (i, 0))],
        out_shape=[jax.ShapeDtypeStruct((m, 512), F32),
                   jax.ShapeDtypeStruct((m, LANE), F32),
                   jax.ShapeDtypeStruct((m, MLA_ROPE), F32)],
        compiler_params=_cparams("parallel"),
    )(p, p, g.reshape(1, -1), *tabs)


def _kv_expand_kernel(lat_ref, krp_ref, wn_ref, wv_ref, gn_ref, gr_ref, k_ref, v_ref):
    lat = lat_ref[...].astype(BF16)
    kr = krp_ref[...]
    ss_r = jnp.sum(kr * kr, axis=-1, keepdims=True)
    gn, gr = gn_ref[...], gr_ref[...]
    for hd in range(MLA_HEADS):
        kn = jnp.dot(lat, wn_ref[:, hd * LANE:(hd + 1) * LANE], preferred_element_type=F32)
        rs = lax.rsqrt((jnp.sum(kn * kn, axis=-1, keepdims=True) + ss_r) * (1.0 / MLA_QK) + EPS)
        lo = hd * MLA_PAD
        k_ref[:, lo:lo + LANE] = (kn * rs * gn).astype(BF16)
        k_ref[:, lo + LANE:lo + MLA_PAD] = (kr * rs * gr).astype(BF16)
        v_ref[:, hd * LANE:(hd + 1) * LANE] = jnp.dot(
            lat, wv_ref[:, hd * LANE:(hd + 1) * LANE], preferred_element_type=F32).astype(BF16)


def mla_kv_expand(lat, krp, src_layer, wn, wv, layer, gn, gr):
    m = lat.shape[1]
    tm = _pick(m, (512, 256, 128))
    vec = pl.BlockSpec((1, LANE), lambda i: (0, 0))
    return pl.pallas_call(
        _kv_expand_kernel, name="mla_kv_expand",
        grid=(m // tm,),
        in_specs=[pl.BlockSpec((None, tm, 512), lambda i: (src_layer, i, 0)),
                  pl.BlockSpec((None, tm, LANE), lambda i: (src_layer, i, 0)),
                  pl.BlockSpec((None,) + wn.shape[1:], lambda i: (layer, 0, 0)),
                  pl.BlockSpec((None,) + wv.shape[1:], lambda i: (layer, 0, 0)), vec, vec],
        out_specs=[pl.BlockSpec((tm, MLA_HEADS * MLA_PAD), lambda i: (i, 0)),
                   pl.BlockSpec((tm, MLA_HEADS * MLA_V), lambda i: (i, 0))],
        out_shape=[jax.ShapeDtypeStruct((m, MLA_HEADS * MLA_PAD), BF16),
                   jax.ShapeDtypeStruct((m, MLA_HEADS * MLA_V), BF16)],
        compiler_params=_cparams("parallel"),
    )(lat, krp, wn, wv, gn, gr)


def _scores(q, kblk):
    return lax.dot_general(q, kblk, (((1,), (1,)), ((), ())), preferred_element_type=F32)


def _softmax_update(s, vblk, carry, mask=None):
    m, l, acc = carry
    if mask is not None:
        s = jnp.where(mask, s, NEG)
    m_new = jnp.maximum(m, jnp.max(s, axis=-1, keepdims=True))
    p = jnp.exp2(s - m_new)
    alpha = jnp.exp2(m - m_new)
    l = alpha * l + jnp.sum(p, axis=-1, keepdims=True)
    acc = alpha * acc + jnp.dot(p.astype(BF16), vblk, preferred_element_type=F32)
    return m_new, l, acc


def _softmax_step(q, kblk, vblk, carry, mask=None):
    return _softmax_update(_scores(q, kblk), vblk, carry, mask)


def _mla_attn_kernel(q_ref, k_ref, v_ref, g_ref, o_ref, *, tq):
    i = pl.program_id(2)
    hq = tq // 2
    qa, qb = q_ref[:hq, :], q_ref[hq:, :]

    def body(j, carry):
        off = pl.multiple_of(j * tq, tq)
        kblk, vblk = k_ref[pl.ds(off, tq), :], v_ref[pl.ds(off, tq), :]
        sa, sb = _scores(qa, kblk), _scores(qb, kblk)
        return _softmax_update(sa, vblk, carry[0]), _softmax_update(sb, vblk, carry[1])

    init = (jnp.full((hq, 1), NEG, F32), jnp.zeros((hq, 1), F32), jnp.zeros((hq, MLA_V), F32))
    ca, cb = lax.fori_loop(0, i, body, (init, init))
    off = pl.multiple_of(i * tq, tq)
    k_lo, v_lo = k_ref[pl.ds(off, hq), :], v_ref[pl.ds(off, hq), :]
    k_hi, v_hi = k_ref[pl.ds(off + hq, hq), :], v_ref[pl.ds(off + hq, hq), :]
    row = lax.broadcasted_iota(jnp.int32, (hq, hq), 0) // CHUNK
    col = lax.broadcasted_iota(jnp.int32, (hq, hq), 1) // CHUNK
    tri = col <= row
    sa, sb_lo, sb_hi = _scores(qa, k_lo), _scores(qb, k_lo), _scores(qb, k_hi)
    _, la, acca = _softmax_update(sa, v_lo, ca, tri)
    cb = _softmax_update(sb_lo, v_lo, cb)
    _, lb, accb = _softmax_update(sb_hi, v_hi, cb, tri)
    o_ref[:hq, :] = (acca / la * _silu(g_ref[:hq, :])).astype(o_ref.dtype)
    o_ref[hq:, :] = (accb / lb * _silu(g_ref[hq:, :])).astype(o_ref.dtype)


def mla_attn_prompt(q, k, v, p, *, bsz, seq):
    tq = _pick(seq, (512, 256))
    nq = seq // tq
    return pl.pallas_call(
        functools.partial(_mla_attn_kernel, tq=tq), name="mla_attn",
        grid=(bsz, MLA_HEADS, nq),
        in_specs=[pl.BlockSpec((tq, MLA_PAD), lambda b, h, i: (b * nq + i, h)),
                  pl.BlockSpec((seq, MLA_PAD), lambda b, h, i: (b, h)),
                  pl.BlockSpec((seq, MLA_V), lambda b, h, i: (b, h)),
                  pl.BlockSpec((tq, MLA_V), lambda b, h, i: (b * nq + i, AB_GA // MLA_V + h))],
        out_specs=pl.BlockSpec((tq, MLA_V), lambda b, h, i: (b * nq + i, h)),
        out_shape=jax.ShapeDtypeStruct((bsz * seq, MLA_HEADS * MLA_V), BF16),
        compiler_params=_cparams("parallel", "parallel", "arbitrary"),
    )(q, k, v, p)


def _mla_attn_step_kernel(q_ref, kc_ref, vc_ref, kn_ref, vn_ref, g_ref, o_ref, *, heads):
    t = q_ref.shape[0]
    init = (jnp.full((t, 1), NEG, F32), jnp.zeros((t, 1), F32), jnp.zeros((t, MLA_V), F32))
    for hd in range(heads):
        ks = slice(hd * MLA_PAD, (hd + 1) * MLA_PAD)
        vs = slice(hd * MLA_V, (hd + 1) * MLA_V)
        q = q_ref[:, ks]
        carry = _softmax_step(q, kc_ref[:, ks], vc_ref[:, vs], init)
        _, l, acc = _softmax_step(q, kn_ref[:, ks], vn_ref[:, vs], carry)
        o_ref[:, vs] = (acc / l * _silu(g_ref[:, vs])).astype(o_ref.dtype)


def mla_attn_step(q, kc, vc, kn, vn, p, *, bsz, t, past):
    heads = 4
    kw, vw = heads * MLA_PAD, heads * MLA_V
    return pl.pallas_call(
        functools.partial(_mla_attn_step_kernel, heads=heads), name="mla_attn_step",
        grid=(bsz, MLA_HEADS // heads),
        in_specs=[pl.BlockSpec((t, kw), lambda b, h: (b, h)),
                  pl.BlockSpec((past, kw), lambda b, h: (b, h)),
                  pl.BlockSpec((past, vw), lambda b, h: (b, h)),
                  pl.BlockSpec((t, kw), lambda b, h: (b, h)),
                  pl.BlockSpec((t, vw), lambda b, h: (b, h)),
                  pl.BlockSpec((t, vw), lambda b, h: (b, AB_GA // vw + h))],
        out_specs=pl.BlockSpec((t, vw), lambda b, h: (b, h)),
        out_shape=jax.ShapeDtypeStruct((bsz * t, MLA_HEADS * MLA_V), BF16),
        compiler_params=_cparams("parallel", "parallel"),
    )(q, kc, vc, kn, vn, p)


def _s5_disc_kernel(lre_ref, lim_ref, ldt_ref, bre_ref, bim_ref, lbre_ref, lbim_ref, bbre_ref, bbim_ref):
    lam_re = jnp.minimum(lre_ref[...], -1e-4)
    lam_im = lim_ref[...]
    dt = jnp.exp(ldt_ref[...])
    mag = jnp.exp(lam_re * dt)
    lb_re = mag * jnp.cos(lam_im * dt)
    lb_im = mag * jnp.sin(lam_im * dt)
    nr, ni = lb_re - 1.0, lb_im
    den = lam_re * lam_re + lam_im * lam_im
    f_re = (nr * lam_re + ni * lam_im) / den
    f_im = (ni * lam_re - nr * lam_im) / den
    b_re, b_im = bre_ref[...], bim_ref[...]
    lbre_ref[...] = lb_re
    lbim_ref[...] = lb_im
    bbre_ref[...] = f_re * b_re - f_im * b_im
    bbim_ref[...] = f_re * b_im + f_im * b_re


def s5_discretize(lam_re, lam_im, log_dt, b_re, b_im):
    g, pdim = lam_re.shape
    bt_re = jnp.swapaxes(b_re, 1, 2)
    bt_im = jnp.swapaxes(b_im, 1, 2)
    s3 = jax.ShapeDtypeStruct((g, 1, pdim), F32)
    sb = jax.ShapeDtypeStruct(bt_re.shape, F32)
    return pl.pallas_call(
        _s5_disc_kernel, name="s5_disc", out_shape=[s3, s3, sb, sb],
    )(lam_re.reshape(g, 1, pdim), lam_im.reshape(g, 1, pdim), log_dt.reshape(g, 1, 1), bt_re, bt_im)


def _s5_kernel(u_ref, h0re_ref, h0im_ref, wbu_ref, lbre_ref, lbim_ref, wcre_ref, wcim_ref, d_ref,
               y_ref, hre_ref, him_ref, ut_ref, sre_ref, sim_ref, *, bsz, tc, nsub, group, unroll):
    sub = 8
    assert bsz in (4, 8) and SSM_SUB * SSM_GROUP == LANE
    rows = bsz * tc
    sw = SSM_SUB * SSM_STATE
    uw = SSM_SUB * SSM_GROUP
    tiles = sw // LANE

    @pl.when(pl.program_id(1) == 0)
    def _():
        hre_ref[...] = h0re_ref[...]
        him_ref[...] = h0im_ref[...]

    for b in range(bsz):
        for s in range(nsub):
            ut_ref[s, pl.ds(b, tc, stride=bsz), :] = u_ref[b, :, s * uw:(s + 1) * uw]

    for s in range(nsub):
        bu = jnp.dot(ut_ref[s].astype(BF16), wbu_ref[s], preferred_element_type=F32)
        for k in range(tiles):
            sre_ref[s * tiles + k] = bu[:, k * LANE:(k + 1) * LANE]
            sim_ref[s * tiles + k] = bu[:, sw + k * LANE:sw + (k + 1) * LANE]

    low = lax.broadcasted_iota(jnp.int32, (sub, LANE), 0) < bsz
    for c0 in range(0, nsub * tiles, group):
        lanes = [slice((c0 + k) * LANE, (c0 + k + 1) * LANE) for k in range(group)]
        ar = [jnp.broadcast_to(lbre_ref[:, ls], (sub, LANE)) for ls in lanes]
        ai = [jnp.broadcast_to(lbim_ref[:, ls], (sub, LANE)) for ls in lanes]

        def step(i, carry, c0=c0, ar=ar, ai=ai):
            rs = pl.ds(pl.multiple_of(i * sub, sub), sub)
            out = []
            for k in range(group):
                hr, hi = carry[2 * k], carry[2 * k + 1]
                xr, xi = sre_ref[c0 + k, rs, :], sim_ref[c0 + k, rs, :]
                if bsz == sub:
                    nr = ar[k] * hr - ai[k] * hi + xr
                    ni = ar[k] * hi + ai[k] * hr + xi
                    sr, si = nr, ni
                else:
                    pr, pi = pltpu.roll(hr, bsz, 0), pltpu.roll(hi, bsz, 0)
                    v1r = ar[k] * pr - ai[k] * pi + xr
                    v1i = ar[k] * pi + ai[k] * pr + xi
                    qr, qi = pltpu.roll(v1r, bsz, 0), pltpu.roll(v1i, bsz, 0)
                    nr = ar[k] * qr - ai[k] * qi + xr
                    ni = ar[k] * qi + ai[k] * qr + xi
                    sr, si = jnp.where(low, v1r, nr), jnp.where(low, v1i, ni)
                sre_ref[c0 + k, rs, :] = sr
                sim_ref[c0 + k, rs, :] = si
                out += [nr, ni]
            return tuple(out)

        def widen(x):
            return x if bsz == sub else jnp.concatenate([x, x], axis=0)

        init = []
        for ls in lanes:
            init += [widen(hre_ref[:, ls]), widen(him_ref[:, ls])]
        fin = lax.fori_loop(0, rows // sub, step, tuple(init), unroll=unroll)
        for k, ls in enumerate(lanes):
            hre_ref[:, ls] = fin[2 * k][sub - bsz:]
            him_ref[:, ls] = fin[2 * k + 1][sub - bsz:]

    d = d_ref[...]
    for s in range(nsub):
        hre = jnp.concatenate([sre_ref[s * tiles + k] for k in range(tiles)], axis=-1).astype(BF16)
        him = jnp.concatenate([sim_ref[s * tiles + k] for k in range(tiles)], axis=-1).astype(BF16)
        y = jnp.dot(hre, wcre_ref[s], preferred_element_type=F32)
        y = y + jnp.dot(him, wcim_ref[s], preferred_element_type=F32)
        y = y + d[:, s * uw:(s + 1) * uw] * ut_ref[s]
        ut_ref[s] = _gelu_tanh(y)
    for b in range(bsz):
        for s in range(nsub):
            y_ref[b, :, s * uw:(s + 1) * uw] = ut_ref[s, pl.ds(b, tc, stride=bsz), :].astype(BF16)


def s5_mixer(p3, h0_re, h0_im, wbu, lb_re, lb_im, wc_re, wc_im, d_skip, *, tc):
    bsz, t, _ = p3.shape
    nsub = 4
    uw = nsub * SSM_SUB * SSM_GROUP
    sw = nsub * SSM_SUB * SSM_STATE
    nblk = wbu.shape[0] // nsub
    bw = uw * nblk
    kern = functools.partial(_s5_kernel, bsz=bsz, tc=tc, nsub=nsub, group=4, unroll=4)
    st_spec = pl.BlockSpec((bsz, sw), lambda g, c: (0, g))
    return pl.pallas_call(
        kern, name="s5_mixer",
        grid=(nblk, t // tc),
        in_specs=[pl.BlockSpec((bsz, tc, uw), lambda g, c: (0, c, AB_U // uw + g)),
                  st_spec, st_spec,
                  pl.BlockSpec((nsub,) + wbu.shape[1:], lambda g, c: (g, 0, 0)),
                  pl.BlockSpec((1, sw), lambda g, c: (0, g)),
                  pl.BlockSpec((1, sw), lambda g, c: (0, g)),
                  pl.BlockSpec((nsub,) + wc_re.shape[1:], lambda g, c: (g, 0, 0)),
                  pl.BlockSpec((nsub,) + wc_im.shape[1:], lambda g, c: (g, 0, 0)),
                  pl.BlockSpec((1, uw), lambda g, c: (0, g))],
        out_specs=[pl.BlockSpec((bsz, tc, uw), lambda g, c: (0, c, g)), st_spec, st_spec],
        out_shape=[jax.ShapeDtypeStruct((bsz, t, bw), BF16),
                   jax.ShapeDtypeStruct(h0_re.shape, F32),
                   jax.ShapeDtypeStruct(h0_im.shape, F32)],
        scratch_shapes=[pltpu.VMEM((nsub, bsz * tc, LANE), F32),
                        pltpu.VMEM((sw // LANE, bsz * tc, LANE), F32),
                        pltpu.VMEM((sw // LANE, bsz * tc, LANE), F32)],
        compiler_params=_cparams("parallel", "arbitrary"),
    )(p3, h0_re, h0_im, wbu, lb_re, lb_im, wc_re, wc_im, d_skip)


def _glu_kernel(y_ref, wa32_ref, wb32_ref, g_ref, o_ref, wa_ref, wb_ref):
    @pl.when(pl.program_id(1) == 0)
    def _():
        wa_ref[...] = wa32_ref[...].astype(BF16)
        wb_ref[...] = wb32_ref[...].astype(BF16)

    y = y_ref[...]
    ga = jnp.dot(y, wa_ref[...], preferred_element_type=F32)
    gb = jnp.dot(y, wb_ref[...], preferred_element_type=F32)
    o_ref[...] = (ga * jax.nn.sigmoid(gb) * _silu(g_ref[...])).astype(o_ref.dtype)


def glu_gate(y, w_glu, layer, p, *, tn=512):
    m, k = y.shape
    n = w_glu.shape[2] // 2
    tm = _pick(m, (1024, 512, 256, 128))
    nj = n // tn
    return pl.pallas_call(
        _glu_kernel, name="glu_gate",
        grid=(nj, m // tm),
        in_specs=[pl.BlockSpec((tm, k), lambda j, i: (i, 0)),
                  pl.BlockSpec((None, k, tn), lambda j, i: (layer, 0, j)),
                  pl.BlockSpec((None, k, tn), lambda j, i: (layer, 0, nj + j)),
                  pl.BlockSpec((tm, tn), lambda j, i: (i, AB_GB // tn + j))],
        out_specs=pl.BlockSpec((tm, tn), lambda j, i: (i, j)),
        out_shape=jax.ShapeDtypeStruct((m, n), BF16),
        scratch_shapes=[pltpu.VMEM((k, tn), BF16), pltpu.VMEM((k, tn), BF16)],
        compiler_params=_cparams("arbitrary", "arbitrary"),
    )(y, w_glu, w_glu, p)


C_Q = 0
C_G = 4096
C_K = 8192
C_V = 8704
KVW = SWA_KV_HEADS * SWA_HEAD_DIM


def _k_norm_kernel(k_ref, g_ref, o_ref):
    k = k_ref[...]
    g = g_ref[...]
    for h in range(SWA_KV_HEADS):
        sl = slice(h * SWA_HEAD_DIM, (h + 1) * SWA_HEAD_DIM)
        o_ref[:, sl] = _rms(k[:, sl], g)


def swa_k_norm(pc, gk):
    m = pc.shape[0]
    tm = _pick(m, (512, 256, 128))
    return pl.pallas_call(
        _k_norm_kernel, name="swa_k_norm",
        grid=(m // tm,),
        in_specs=[pl.BlockSpec((tm, KVW), lambda i: (i, C_K // KVW)),
                  pl.BlockSpec((1, SWA_HEAD_DIM), lambda i: (0, 0))],
        out_specs=pl.BlockSpec((tm, KVW), lambda i: (i, 0)),
        out_shape=jax.ShapeDtypeStruct((m, KVW), F32),
        compiler_params=_cparams("parallel"),
    )(pc, gk.reshape(1, -1))


def _swa_prompt_kernel(sink_ref, q_ref, kp_ref, kc_ref, vp_ref, vc_ref, g_ref, gq_ref, b_ref, o_ref, *, pair_batch):
    nk = b_ref.shape[-1]
    pad = nk - kp_ref.shape[0] - kc_ref.shape[0]
    fill = [jnp.zeros((pad, KVW), F32)] if pad else []
    kcat = jnp.concatenate([kp_ref[...], kc_ref[...]] + fill, axis=0)
    vcat = jnp.concatenate([vp_ref[...], vc_ref[...]] + fill, axis=0)
    seg_r = lax.broadcasted_iota(jnp.int32, (LANE, LANE), 0) // SWA_HEAD_DIM
    seg_c = lax.broadcasted_iota(jnp.int32, (LANE, LANE), 1) // SWA_HEAD_DIM
    seg_ones = (seg_r == seg_c).astype(BF16)
    left = lax.broadcasted_iota(jnp.int32, (1, LANE), 1) < SWA_HEAD_DIM
    gq = gq_ref[...] * (SWA_HEAD_DIM ** -0.5 * LOG2E)
    nt = (((1,), (1,)), ((), ()))

    def halves(slab, half):
        own = jnp.where(left if half == 0 else jnp.logical_not(left), slab, 0.0)
        other = pltpu.roll(own, SWA_HEAD_DIM, 1)
        lo, hi = (own, other) if half == 0 else (other, own)
        return lo.astype(BF16), hi.astype(BF16)

    kv = {}

    def kv_halves(kvh):
        if kvh not in kv:
            ks = slice((kvh // 2) * LANE, (kvh // 2 + 1) * LANE)
            kv.clear()
            kv[kvh] = halves(kcat[:, ks], kvh % 2) + halves(vcat[:, ks], kvh % 2)
        return kv[kvh]

    def scores(pair):
        sl = slice(pair * LANE, (pair + 1) * LANE)
        k_lo, k_hi, _, _ = kv_halves(pair // (SWA_GQ // 2))
        q = q_ref[:, sl]
        x2 = q * q
        x2_hi = x2.astype(BF16)
        x2_lo = (x2 - x2_hi.astype(F32)).astype(BF16)
        ss = (jnp.dot(x2_hi, seg_ones, preferred_element_type=F32)
              + jnp.dot(x2_lo, seg_ones, preferred_element_type=F32))
        qn = (q * lax.rsqrt(ss * (1.0 / SWA_HEAD_DIM) + EPS) * gq).astype(BF16)
        return [lax.dot_general(qn, k_lo, nt, preferred_element_type=F32) + b_ref[2 * pair],
                lax.dot_general(qn, k_hi, nt, preferred_element_type=F32) + b_ref[2 * pair + 1]]

    def finish(pairs, ss, vhs):
        sinks = [sink_ref[2 * p + j] * LOG2E for p in pairs for j in range(2)]
        flat = [s for pr in ss for s in pr]
        ms = [jnp.maximum(jnp.max(s, axis=-1, keepdims=True), sk) for s, sk in zip(flat, sinks)]
        es = [jnp.exp2(s - m) for s, m in zip(flat, ms)]
        ds = [jnp.sum(e, axis=-1, keepdims=True) + jnp.exp2(sk - m) for e, m, sk in zip(es, ms, sinks)]
        for i, p in enumerate(pairs):
            v_lo, v_hi = vhs[i]
            o = (jnp.dot(es[2 * i].astype(BF16), v_lo, preferred_element_type=F32)
                 + jnp.dot(es[2 * i + 1].astype(BF16), v_hi, preferred_element_type=F32))
            inv = jnp.where(left, 1.0 / ds[2 * i], 1.0 / ds[2 * i + 1])
            sl = slice(p * LANE, (p + 1) * LANE)
            o_ref[:, sl] = (o * inv * _silu(g_ref[:, sl])).astype(o_ref.dtype)

    npairs = SWA_HEADS // 2
    batches = [list(range(b, b + pair_batch)) for b in range(0, npairs, pair_batch)]
    nxt = ([scores(p) for p in batches[0]], [kv_halves(p // (SWA_GQ // 2))[2:] for p in batches[0]])
    for bi, pairs in enumerate(batches):
        cur = nxt
        if bi + 1 < len(batches):
            nb = batches[bi + 1]
            nxt = ([scores(p) for p in nb], [kv_halves(p // (SWA_GQ // 2))[2:] for p in nb])
        finish(pairs, *cur)


def swa_attn_prompt(pc, kn, gq, sinks, bias, *, bsz, seq):
    tq = WINDOW
    nq = seq // tq
    dm = SWA_HEADS * SWA_HEAD_DIM
    prev = lambda b, i: (b * nq + jnp.maximum(i - 1, 0), 0)
    cur = lambda b, i: (b * nq + i, 0)
    return pl.pallas_call(
        functools.partial(_swa_prompt_kernel, pair_batch=2), name="swa_attn",
        grid=(bsz, nq),
        in_specs=[pl.BlockSpec(memory_space=pltpu.SMEM),
                  pl.BlockSpec((tq, dm), lambda b, i: (b * nq + i, C_Q // dm)),
                  pl.BlockSpec((tq, KVW), prev), pl.BlockSpec((tq, KVW), cur),
                  pl.BlockSpec((tq, KVW), lambda b, i: (b * nq + jnp.maximum(i - 1, 0), C_V // KVW)),
                  pl.BlockSpec((tq, KVW), lambda b, i: (b * nq + i, C_V // KVW)),
                  pl.BlockSpec((tq, dm), lambda b, i: (b * nq + i, C_G // dm)),
                  pl.BlockSpec((1, LANE), lambda b, i: (0, 0)),
                  pl.BlockSpec((None,) + bias.shape[1:], lambda b, i: (jnp.minimum(i, 1), 0, 0, 0))],
        out_specs=pl.BlockSpec((tq, dm), cur),
        out_shape=jax.ShapeDtypeStruct((bsz * seq, dm), BF16),
        compiler_params=_cparams("parallel", "arbitrary"),
    )(sinks, pc, kn, kn, pc, pc, pc, jnp.tile(gq.reshape(1, -1), (1, 2)), bias)


def swa_attn_step(pc, k_cache, kn, v_cache, layer, gq, sinks, bias, *, bsz, t):
    dm = SWA_HEADS * SWA_HEAD_DIM
    row = lambda b: (b, 0)
    cache = pl.BlockSpec((None, WINDOW, KVW), lambda b: (layer, b, 0))
    return pl.pallas_call(
        functools.partial(_swa_prompt_kernel, pair_batch=4), name="swa_attn_step",
        grid=(bsz,),
        in_specs=[pl.BlockSpec(memory_space=pltpu.SMEM),
                  pl.BlockSpec((t, dm), lambda b: (b, C_Q // dm)),
                  cache, pl.BlockSpec((t, KVW), row), cache,
                  pl.BlockSpec((t, KVW), lambda b: (b, C_V // KVW)),
                  pl.BlockSpec((t, dm), lambda b: (b, C_G // dm)),
                  pl.BlockSpec((1, LANE), lambda b: (0, 0)),
                  pl.BlockSpec(bias.shape, lambda b: (0, 0, 0))],
        out_specs=pl.BlockSpec((t, dm), row),
        out_shape=jax.ShapeDtypeStruct((bsz * t, dm), BF16),
        compiler_params=_cparams("parallel"),
    )(sinks, pc, k_cache, kn, v_cache, pc, pc, jnp.tile(gq.reshape(1, -1), (1, 2)), bias)


def _rope_tables(pos, reps=1):
    half = MLA_ROPE // 2
    inv = ROPE_THETA ** (-jnp.arange(half, dtype=F32) / half)
    ang = pos.astype(F32)[:, None] * inv
    cos, sin = jnp.cos(ang), jnp.sin(ang)
    z = jnp.zeros_like(cos)
    tabs = (jnp.concatenate([cos, cos, z, z], -1), jnp.concatenate([-sin, z, z, z], -1),
            jnp.concatenate([z, sin, z, z], -1))
    return tuple(jnp.tile(t, (reps, 1)) for t in tabs)


def _rel_bucket(rel):
    nb = REL_BUCKETS // 2
    max_exact = nb // 2
    ret = jnp.where(rel > 0, nb, 0)
    n = jnp.abs(rel)
    nf = jnp.maximum(n, 1).astype(F32)
    large = max_exact + (jnp.log(nf / max_exact) / math.log(REL_MAX_DIST / max_exact)
                         * (nb - max_exact)).astype(jnp.int32)
    large = jnp.minimum(large, nb - 1)
    return ret + jnp.where(n < max_exact, n, large)


def _rel_bias(table, rel, mask):
    onehot = (_rel_bucket(rel)[:, :, None] == jnp.arange(REL_BUCKETS)).astype(F32)
    b = jnp.einsum("qkb,bh->hqk", onehot, table.astype(F32) * LOG2E, precision=lax.Precision.HIGHEST)
    return jnp.where(mask[None], b, NEG)


def _prep_ab_weights(w_in, w_uq, w_ukv):
    nl, d, _ = w_in.shape
    wb = w_in.astype(BF16)
    w_in_r = jnp.concatenate([wb[:, :, 0:768], wb[:, :, 1280:1344],
                              jnp.zeros((nl, d, AB_CKV - AB_KR - MLA_ROPE), BF16),
                              wb[:, :, 768:1280], wb[:, :, 1344:]], axis=2)
    uq = w_uq.astype(BF16).reshape(nl, w_uq.shape[1], MLA_HEADS, MLA_QK)
    uq = jnp.pad(uq, ((0, 0), (0, 0), (0, 0), (0, MLA_PAD - MLA_QK))).reshape(nl, w_uq.shape[1], -1)
    ukv = w_ukv.astype(BF16).reshape(nl, w_ukv.shape[1], MLA_HEADS, MLA_NOPE + MLA_V)
    wn = ukv[:, :, :, :MLA_NOPE].reshape(nl, w_ukv.shape[1], -1)
    wv = ukv[:, :, :, MLA_NOPE:].reshape(nl, w_ukv.shape[1], -1)
    return dict(w_in=w_in_r, w_uq=uq, wn=wn, wv=wv)


def _head_gains(g):
    return g[:MLA_NOPE].reshape(1, LANE), jnp.pad(g[MLA_NOPE:], (0, LANE - MLA_ROPE)).reshape(1, LANE)


def _prep_s5(lam_re, lam_im, log_dt, b_re, b_im, c_re, c_im):
    g = lam_re.shape[0]
    lb_re, lb_im, bbt_re, bbt_im = s5_discretize(lam_re, lam_im, log_dt, b_re, b_im)
    eye = jnp.eye(SSM_SUB, dtype=F32)
    ns = g // SSM_SUB
    bb = jnp.stack([bbt_re, bbt_im]).reshape(2, ns, SSM_SUB, SSM_GROUP, SSM_STATE)
    wbu = bb[:, :, :, :, None, :] * eye[None, None, :, None, :, None]
    wbu = jnp.transpose(wbu, (1, 2, 3, 0, 4, 5)).reshape(ns, SSM_SUB * SSM_GROUP, 2 * SSM_SUB * SSM_STATE)

    def readout(c):
        c4 = c.astype(F32).reshape(ns, SSM_SUB, SSM_GROUP, SSM_STATE)
        w = c4[:, :, :, None, :] * eye[None, :, None, :, None]
        return jnp.transpose(w, (0, 3, 4, 1, 2)).reshape(ns, SSM_SUB * SSM_STATE, SSM_SUB * SSM_GROUP)

    return dict(wbu=wbu.astype(BF16), lb_re=lb_re.reshape(1, -1), lb_im=lb_im.reshape(1, -1),
                wc_re=readout(c_re).astype(BF16), wc_im=readout(-c_im).astype(BF16))


def _ab_layer(x, i, w, prm, s5w, tabs, *, bsz, seq, past):
    m = x.shape[0]
    gqn, gqr = _head_gains(prm["q_norm"][i])
    gkn, gkr = _head_gains(prm["k_norm"][i])
    p = norm_matmul(x, prm["norm"][i], w["w_in"], i, tn=768)
    q = mla_q_prep(p, prm["q_lora_norm"][i], w["w_uq"], i, tabs, gqn, gqr)
    lat, krp, kro = mla_latent(p, prm["kv_lora_norm"][i], tabs)
    k, v = mla_kv_expand(lat[None], krp[None], 0, w["wn"], w["wv"], i, gkn, gkr)
    p3 = p.reshape(bsz, seq, AB_COLS)
    gdim = s5w["lb_re"].shape[1]
    d_skip = prm["d_skip"][i].reshape(1, -1)
    s5args = (s5w["wbu"], s5w["lb_re"], s5w["lb_im"], s5w["wc_re"], s5w["wc_im"], d_skip)
    if past is None:
        a_out = mla_attn_prompt(q, k, v, p, bsz=bsz, seq=seq)
        h0 = jnp.zeros((bsz, gdim), F32)
        y, h_re, h_im = s5_mixer(p3, h0, h0, *s5args, tc=2 * CHUNK)
    else:
        lat_c, kr_c, h0_re, h0_im = past
        kc, vc = mla_kv_expand(lat_c, kr_c, i, w["wn"], w["wv"], i, gkn, gkr)
        a_out = mla_attn_step(q, kc, vc, k, v, p, bsz=bsz, t=seq, past=lat_c.shape[1] // bsz)
        y, h_re, h_im = s5_mixer(p3, h0_re.reshape(bsz, gdim), h0_im.reshape(bsz, gdim), *s5args, tc=seq)
    b_out = glu_gate(y.reshape(m, -1), prm["w_glu"], i, p)
    x_new = out_proj([a_out, b_out], prm["w_out"], i, x)
    return x_new, (lat, kro, h_re, h_im)


def _c_in_col(j):
    return jnp.where(j < 8, j, jnp.where(j < 10, j + 8, j - 2))


def _c_layer(x, i, prm, biases, *, bsz, seq, past):
    assert prm["w_in"].shape[2] == C_V + KVW and C_G == 8 * KVW and C_K == 16 * KVW
    pc = wcast_matmul(rms_cast(x, prm["norm"][i]), prm["w_in"], i, _c_in_col, tn=KVW)
    gq, gk, sinks = prm["q_norm"][i], prm["k_norm"][i], prm["sinks"][i]
    kn = swa_k_norm(pc, gk)
    v = pc[:, C_V:C_V + KVW]
    if past is None:
        a = swa_attn_prompt(pc, kn, gq, sinks, *biases, bsz=bsz, seq=seq)
        k_buf = kn.reshape(bsz, seq, KVW)[:, -WINDOW:]
        v_buf = v.reshape(bsz, seq, KVW)[:, -WINDOW:]
    else:
        k_c, v_c = past
        a = swa_attn_step(pc, k_c, kn, v_c, i, gq, sinks, *biases, bsz=bsz, t=seq)
        k_buf = jnp.concatenate([k_c[i].reshape(bsz, WINDOW, KVW), kn.reshape(bsz, seq, KVW)], axis=1)[:, -WINDOW:]
        v_buf = jnp.concatenate([v_c[i].reshape(bsz, WINDOW, KVW), v.reshape(bsz, seq, KVW)], axis=1)[:, -WINDOW:]
    x_new = out_proj([a], prm["w_out"], i, x)
    shp = (bsz, WINDOW, SWA_KV_HEADS, SWA_HEAD_DIM)
    return x_new, (k_buf.reshape(shp), v_buf.reshape(shp))


def kernel(x_prompt, x_sample, cache_mla_latent, cache_mla_krope, state_ssm_re, state_ssm_im, cache_swa_k, cache_swa_v, rel_bias_table, ab_norm, ab_w_in, ab_q_lora_norm, ab_kv_lora_norm, ab_w_uq, ab_w_ukv, ab_q_norm, ab_k_norm, ssm_lambda_re, ssm_lambda_im, ssm_b_re, ssm_b_im, ssm_c_re, ssm_c_im, ssm_log_dt, ssm_d, ssm_w_glu, ab_w_out, c_norm, c_w_in, c_q_norm, c_k_norm, c_sinks, c_w_out):
    bp, lp, d = x_prompt.shape
    bs, ts, _ = x_sample.shape
    past_len = cache_mla_latent.shape[2]
    depth = ab_norm.shape[0] + c_norm.shape[0]
    assert lp % 256 == 0 and (bs * ts) % 128 == 0 and past_len % 512 == 0 and d == SWA_HEADS * SWA_HEAD_DIM

    q_pos = past_len + np.arange(ts)
    assert np.all((np.arange(past_len + ts)[None, :] // CHUNK) <= (q_pos[:, None] // CHUNK))
    k_pos = past_len - WINDOW + np.arange(WINDOW + ts)
    qch, kch = q_pos[:, None] // CHUNK, k_pos[None, :] // CHUNK
    assert np.all((kch <= qch) & (kch >= qch - WINDOW // CHUNK))

    tabs_p = _rope_tables(jnp.arange(lp))
    tabs_s = _rope_tables(past_len + jnp.arange(ts), reps=bs)

    qi = np.arange(WINDOW)[:, None]
    kj = np.arange(2 * WINDOW)[None, :]
    band = (kj // CHUNK - 2 <= qi // CHUNK) & (kj // CHUNK >= qi // CHUNK)
    bias_p = _rel_bias(rel_bias_table, jnp.asarray(kj - WINDOW - qi), jnp.asarray(band))
    biases_p = (jnp.stack([jnp.where(jnp.asarray(kj < WINDOW)[None], NEG, bias_p), bias_p]),)
    kj_s = np.arange(2 * WINDOW)[None, :]
    rel_s = np.where(kj_s < WINDOW + ts, (past_len - WINDOW + kj_s) - q_pos[:, None], 0)
    biases_s = (_rel_bias(rel_bias_table, jnp.asarray(rel_s),
                          jnp.asarray(np.broadcast_to(kj_s < WINDOW + ts, rel_s.shape))),)

    xp = x_prompt.reshape(bp * lp, d)
    xs = x_sample.reshape(bs * ts, d)
    outs_p = [[] for _ in range(6)]
    outs_s = [[] for _ in range(6)]
    n_ab = ab_norm.shape[0]
    ab_w = _prep_ab_weights(ab_w_in, ab_w_uq, ab_w_ukv)
    ab_prm = dict(norm=ab_norm, q_lora_norm=ab_q_lora_norm, kv_lora_norm=ab_kv_lora_norm, q_norm=ab_q_norm,
                  k_norm=ab_k_norm, d_skip=ssm_d, w_glu=ssm_w_glu, w_out=ab_w_out)
    c_prm = dict(norm=c_norm, w_in=c_w_in, q_norm=c_q_norm, k_norm=c_k_norm, sinks=c_sinks, w_out=c_w_out)
    lat_cache = cache_mla_latent.reshape(n_ab, bs * past_len, -1)
    kr_cache = jnp.pad(cache_mla_krope.reshape(n_ab, bs * past_len, -1), ((0, 0), (0, 0), (0, LANE - MLA_ROPE)))
    swa_past = (cache_swa_k.reshape(-1, bs * WINDOW, KVW), cache_swa_v.reshape(-1, bs * WINDOW, KVW))
    for layer in range(depth):
        i = layer // 2
        if layer % 2 == 0:
            s5w = _prep_s5(ssm_lambda_re[i], ssm_lambda_im[i], ssm_log_dt[i], ssm_b_re[i], ssm_b_im[i],
                           ssm_c_re[i], ssm_c_im[i])
            xp, (a, b, c, e) = _ab_layer(xp, i, ab_w, ab_prm, s5w, tabs_p, bsz=bp, seq=lp, past=None)
            outs_p[0].append(a.reshape(bp, lp, -1)); outs_p[1].append(b.reshape(bp, lp, -1))
            outs_p[2].append(c.reshape(bp, -1, SSM_STATE)); outs_p[3].append(e.reshape(bp, -1, SSM_STATE))
            past = (lat_cache, kr_cache, state_ssm_re[i], state_ssm_im[i])
            xs, (a, b, c, e) = _ab_layer(xs, i, ab_w, ab_prm, s5w, tabs_s, bsz=bs, seq=ts, past=past)
            outs_s[0].append(a.reshape(bs, ts, -1)); outs_s[1].append(b.reshape(bs, ts, -1))
            outs_s[2].append(c.reshape(bs, -1, SSM_STATE)); outs_s[3].append(e.reshape(bs, -1, SSM_STATE))
        else:
            xp, (a, b) = _c_layer(xp, i, c_prm, biases_p, bsz=bp, seq=lp, past=None)
            outs_p[4].append(a); outs_p[5].append(b)
            xs, (a, b) = _c_layer(xs, i, c_prm, biases_s, bsz=bs, seq=ts, past=swa_past)
            outs_s[4].append(a); outs_s[5].append(b)
    return (xp.reshape(bp, lp, d), xs.reshape(bs, ts, d),
            *[jnp.stack(o) for o in outs_p], *[jnp.stack(o) for o in outs_s])
```

```python
import functools
import math

import jax
import jax.numpy as jnp
import numpy as np
from jax import lax
from jax.experimental import pallas as pl
from jax.experimental.pallas import tpu as pltpu

F32 = jnp.float32
BF16 = jnp.bfloat16

EPS = 1e-6
CHUNK = 64
ROPE_THETA = 10000.0
MLA_HEADS = 16
MLA_NOPE = 128
MLA_ROPE = 64
MLA_QK = MLA_NOPE + MLA_ROPE
MLA_V = 128
MLA_PAD = 256
SSM_GROUP = 16
SSM_STATE = 64
SSM_SUB = 8
SWA_HEADS = 64
SWA_KV_HEADS = 8
SWA_GQ = SWA_HEADS // SWA_KV_HEADS
SWA_HEAD_DIM = 64
WINDOW = 128
REL_BUCKETS = 32
REL_MAX_DIST = 128
NEG = -1e30
LOG2E = math.log2(math.e)
LANE = 128
VMEM_LIMIT = 56 * 1024 * 1024

AB_CQ = 0
AB_KR = 768
AB_CKV = 1024
AB_GA = 1536
AB_U = 3584
AB_GB = 5632
AB_COLS = 7680


def _cparams(*sem):
    return pltpu.CompilerParams(dimension_semantics=sem, vmem_limit_bytes=VMEM_LIMIT)


def _rms(x, g):
    return x * lax.rsqrt(jnp.mean(x * x, axis=-1, keepdims=True) + EPS) * g


def _silu(x):
    return x * jax.nn.sigmoid(x)


def _gelu_tanh(x):
    return 0.5 * x * (1.0 + jnp.tanh(math.sqrt(2.0 / math.pi) * (x + 0.044715 * (x * x * x))))


def _pick(n, prefs):
    for p in prefs:
        if n % p == 0:
            return p
    raise ValueError(f"no tile for {n} in {prefs}")


def _norm_mm_kernel(x_ref, g_ref, w_ref, o_ref, h_ref):
    @pl.when(pl.program_id(1) == 0)
    def _():
        h_ref[...] = _rms(x_ref[...], g_ref[...]).astype(BF16)

    o_ref[...] = jnp.dot(h_ref[...], w_ref[...], preferred_element_type=F32).astype(o_ref.dtype)


def norm_matmul(x, g, w, layer, *, tn, out_dtype=F32):
    m, k = x.shape
    n = w.shape[2]
    tm = _pick(m, (512, 256, 128))
    return pl.pallas_call(
        _norm_mm_kernel, name="norm_mm",
        grid=(m // tm, n // tn),
        in_specs=[pl.BlockSpec((tm, k), lambda i, j: (i, 0)),
                  pl.BlockSpec((1, k), lambda i, j: (0, 0)),
                  pl.BlockSpec((None, k, tn), lambda i, j: (layer, 0, j))],
        out_specs=pl.BlockSpec((tm, tn), lambda i, j: (i, j)),
        out_shape=jax.ShapeDtypeStruct((m, n), out_dtype),
        scratch_shapes=[pltpu.VMEM((tm, k), BF16)],
        compiler_params=_cparams("parallel", "arbitrary"),
    )(x, g.reshape(1, k), w)


def _rms_cast_kernel(x_ref, g_ref, o_ref):
    o_ref[...] = _rms(x_ref[...], g_ref[...]).astype(o_ref.dtype)


def rms_cast(x, g):
    m, k = x.shape
    tm = _pick(m, (512, 256, 128))
    return pl.pallas_call(
        _rms_cast_kernel, name="rms_cast",
        grid=(m // tm,),
        in_specs=[pl.BlockSpec((tm, k), lambda i: (i, 0)), pl.BlockSpec((1, k), lambda i: (0, 0))],
        out_specs=pl.BlockSpec((tm, k), lambda i: (i, 0)),
        out_shape=jax.ShapeDtypeStruct((m, k), BF16),
        compiler_params=_cparams("parallel"),
    )(x, g.reshape(1, k))


def _wcast_mm_kernel(h_ref, w_ref, o_ref, wb_ref):
    @pl.when(pl.program_id(1) == 0)
    def _():
        wb_ref[...] = w_ref[...].astype(BF16)

    o_ref[...] = jnp.dot(h_ref[...], wb_ref[...], preferred_element_type=F32).astype(o_ref.dtype)


def wcast_matmul(h, w, layer, out_col, *, tn):
    m, k = h.shape
    n = w.shape[2]
    tm = _pick(m, (1024, 512, 256, 128))
    return pl.pallas_call(
        _wcast_mm_kernel, name="wcast_mm",
        grid=(n // tn, m // tm),
        in_specs=[pl.BlockSpec((tm, k), lambda j, i: (i, 0)),
                  pl.BlockSpec((None, k, tn), lambda j, i: (layer, 0, j))],
        out_specs=pl.BlockSpec((tm, tn), lambda j, i: (i, out_col(j))),
        out_shape=jax.ShapeDtypeStruct((m, n), F32),
        scratch_shapes=[pltpu.VMEM((k, tn), BF16)],
        compiler_params=_cparams("arbitrary", "arbitrary"),
    )(h, w)


def _out_proj_kernel(*refs, n_lhs):
    lhs = refs[:n_lhs]
    w_ref, x_ref, o_ref, wb_ref = refs[n_lhs:]

    @pl.when(pl.program_id(1) == 0)
    def _():
        wb_ref[...] = w_ref[...].astype(BF16)

    acc = x_ref[...]
    off = 0
    for a in lhs:
        kk = a.shape[1]
        acc = acc + jnp.dot(a[...], wb_ref[off:off + kk, :], preferred_element_type=F32)
        off += kk
    o_ref[...] = acc


def out_proj(lhs, w, layer, x, *, tn=512):
    m, n = x.shape
    k = w.shape[1]
    tm = _pick(m, (1024, 512, 256, 128))
    in_specs = [pl.BlockSpec((tm, a.shape[1]), lambda j, i: (i, 0)) for a in lhs]
    in_specs += [pl.BlockSpec((None, k, tn), lambda j, i: (layer, 0, j)),
                 pl.BlockSpec((tm, tn), lambda j, i: (i, j))]
    return pl.pallas_call(
        functools.partial(_out_proj_kernel, n_lhs=len(lhs)), name="out_proj",
        grid=(n // tn, m // tm),
        in_specs=in_specs,
        out_specs=pl.BlockSpec((tm, tn), lambda j, i: (i, j)),
        out_shape=jax.ShapeDtypeStruct((m, n), F32),
        scratch_shapes=[pltpu.VMEM((k, tn), BF16)],
        compiler_params=_cparams("arbitrary", "arbitrary"),
    )(*lhs, w, x)


def _rope_slab(r, c, s1, s2):
    return r * c + pltpu.roll(r, 96, 1) * s1 + pltpu.roll(r, 32, 1) * s2


def _q_prep_kernel(cq_ref, g_ref, w_ref, c_ref, s1_ref, s2_ref, gn_ref, gr_ref, q_ref):
    h = _rms(cq_ref[...], g_ref[...]).astype(BF16)
    c, s1, s2 = c_ref[...], s1_ref[...], s2_ref[...]
    scale = MLA_QK ** -0.5 * math.log2(math.e)
    gn = gn_ref[...] * scale
    gr = gr_ref[...] * scale
    for hd in range(MLA_HEADS):
        lo = hd * MLA_PAD
        nope = jnp.dot(h, w_ref[:, lo:lo + LANE], preferred_element_type=F32)
        rr = _rope_slab(jnp.dot(h, w_ref[:, lo + LANE:lo + MLA_PAD], preferred_element_type=F32), c, s1, s2)
        ss = jnp.sum(nope * nope, axis=-1, keepdims=True) + jnp.sum(rr * rr, axis=-1, keepdims=True)
        rs = lax.rsqrt(ss * (1.0 / MLA_QK) + EPS)
        q_ref[:, lo:lo + LANE] = (nope * rs * gn).astype(BF16)
        q_ref[:, lo + LANE:lo + MLA_PAD] = (rr * rs * gr).astype(BF16)


def mla_q_prep(p, g, w_uq, layer, tabs, gn, gr):
    m = p.shape[0]
    period = tabs[0].shape[0]
    tm = _pick(period, (256, 128))
    nb = period // tm
    tab_spec = pl.BlockSpec((tm, LANE), lambda i: (i % nb, 0))
    vec = lambda n: pl.BlockSpec((1, n), lambda i: (0, 0))
    return pl.pallas_call(
        _q_prep_kernel, name="mla_q_prep",
        grid=(m // tm,),
        in_specs=[pl.BlockSpec((tm, 768), lambda i: (i, 0)), vec(768),
                  pl.BlockSpec((None,) + w_uq.shape[1:], lambda i: (layer, 0, 0)),
                  tab_spec, tab_spec, tab_spec, vec(LANE), vec(LANE)],
        out_specs=pl.BlockSpec((tm, MLA_HEADS * MLA_PAD), lambda i: (i, 0)),
        out_shape=jax.ShapeDtypeStruct((m, MLA_HEADS * MLA_PAD), BF16),
        compiler_params=_cparams("parallel"),
    )(p, g.reshape(1, -1), w_uq, *tabs, gn, gr)


def _latent_kernel(ckv_ref, kr_ref, g_ref, c_ref, s1_ref, s2_ref, lat_ref, krp_ref, kro_ref):
    lat_ref[...] = _rms(ckv_ref[...], g_ref[...])
    rr = _rope_slab(kr_ref[...], c_ref[...], s1_ref[...], s2_ref[...])
    krp_ref[...] = rr
    kro_ref[...] = rr[:, :MLA_ROPE]


def mla_latent(p, g, tabs):
    m = p.shape[0]
    period = tabs[0].shape[0]
    tm = _pick(period, (512, 256, 128))
    nb = period // tm
    tab_spec = pl.BlockSpec((tm, LANE), lambda i: (i % nb, 0))
    return pl.pallas_call(
        _latent_kernel, name="mla_latent",
        grid=(m // tm,),
        in_specs=[pl.BlockSpec((tm, 512), lambda i: (i, AB_CKV // 512)),
                  pl.BlockSpec((tm, LANE), lambda i: (i, AB_KR // LANE)),
                  pl.BlockSpec((1, 512), lambda i: (0, 0)),
                  tab_spec, tab_spec, tab_spec],
        out_specs=[pl.BlockSpec((tm, 512), lambda i: (i, 0)),
                   pl.BlockSpec((tm, LANE), lambda i: (i, 0)),
                   pl.BlockSpec((tm, MLA_ROPE), lambda i: (i, 0))],
        out_shape=[jax.ShapeDtypeStruct((m, 512), F32),
                   jax.ShapeDtypeStruct((m, LANE), F32),
                   jax.ShapeDtypeStruct((m, MLA_ROPE), F32)],
        compiler_params=_cparams("parallel"),
    )(p, p, g.reshape(1, -1), *tabs)


def _kv_expand_kernel(lat_ref, krp_ref, wn_ref, wv_ref, gn_ref, gr_ref, k_ref, v_ref):
    lat = lat_ref[...].astype(BF16)
    kr = krp_ref[...]
    ss_r = jnp.sum(kr * kr, axis=-1, keepdims=True)
    gn, gr = gn_ref[...], gr_ref[...]
    for hd in range(MLA_HEADS):
        kn = jnp.dot(lat, wn_ref[:, hd * LANE:(hd + 1) * LANE], preferred_element_type=F32)
        rs = lax.rsqrt((jnp.sum(kn * kn, axis=-1, keepdims=True) + ss_r) * (1.0 / MLA_QK) + EPS)
        lo = hd * MLA_PAD
        k_ref[:, lo:lo + LANE] = (kn * rs * gn).astype(BF16)
        k_ref[:, lo + LANE:lo + MLA_PAD] = (kr * rs * gr).astype(BF16)
        v_ref[:, hd * LANE:(hd + 1) * LANE] = jnp.dot(
            lat, wv_ref[:, hd * LANE:(hd + 1) * LANE], preferred_element_type=F32).astype(BF16)


def mla_kv_expand(lat, krp, src_layer, wn, wv, layer, gn, gr):
    m = lat.shape[1]
    tm = _pick(m, (512, 256, 128))
    vec = pl.BlockSpec((1, LANE), lambda i: (0, 0))
    return pl.pallas_call(
        _kv_expand_kernel, name="mla_kv_expand",
        grid=(m // tm,),
        in_specs=[pl.BlockSpec((None, tm, 512), lambda i: (src_layer, i, 0)),
                  pl.BlockSpec((None, tm, LANE), lambda i: (src_layer, i, 0)),
                  pl.BlockSpec((None,) + wn.shape[1:], lambda i: (layer, 0, 0)),
                  pl.BlockSpec((None,) + wv.shape[1:], lambda i: (layer, 0, 0)), vec, vec],
        out_specs=[pl.BlockSpec((tm, MLA_HEADS * MLA_PAD), lambda i: (i, 0)),
                   pl.BlockSpec((tm, MLA_HEADS * MLA_V), lambda i: (i, 0))],
        out_shape=[jax.ShapeDtypeStruct((m, MLA_HEADS * MLA_PAD), BF16),
                   jax.ShapeDtypeStruct((m, MLA_HEADS * MLA_V), BF16)],
        compiler_params=_cparams("parallel"),
    )(lat, krp, wn, wv, gn, gr)


def _scores(q, kblk):
    return lax.dot_general(q, kblk, (((1,), (1,)), ((), ())), preferred_element_type=F32)


def _softmax_update(s, vblk, carry, mask=None):
    m, l, acc = carry
    if mask is not None:
        s = jnp.where(mask, s, NEG)
    m_new = jnp.maximum(m, jnp.max(s, axis=-1, keepdims=True))
    p = jnp.exp2(s - m_new)
    alpha = jnp.exp2(m - m_new)
    l = alpha * l + jnp.sum(p, axis=-1, keepdims=True)
    acc = alpha * acc + jnp.dot(p.astype(BF16), vblk, preferred_element_type=F32)
    return m_new, l, acc


def _softmax_step(q, kblk, vblk, carry, mask=None):
    return _softmax_update(_scores(q, kblk), vblk, carry, mask)


def _mla_attn_kernel(q_ref, k_ref, v_ref, g_ref, o_ref, *, tq):
    i = pl.program_id(2)
    hq = tq // 2
    qa, qb = q_ref[:hq, :], q_ref[hq:, :]

    def body(j, carry):
        off = pl.multiple_of(j * tq, tq)
        kblk, vblk = k_ref[pl.ds(off, tq), :], v_ref[pl.ds(off, tq), :]
        sa, sb = _scores(qa, kblk), _scores(qb, kblk)
        return _softmax_update(sa, vblk, carry[0]), _softmax_update(sb, vblk, carry[1])

    init = (jnp.full((hq, 1), NEG, F32), jnp.zeros((hq, 1), F32), jnp.zeros((hq, MLA_V), F32))
    ca, cb = lax.fori_loop(0, i, body, (init, init))
    off = pl.multiple_of(i * tq, tq)
    k_lo, v_lo = k_ref[pl.ds(off, hq), :], v_ref[pl.ds(off, hq), :]
    k_hi, v_hi = k_ref[pl.ds(off + hq, hq), :], v_ref[pl.ds(off + hq, hq), :]
    row = lax.broadcasted_iota(jnp.int32, (hq, hq), 0) // CHUNK
    col = lax.broadcasted_iota(jnp.int32, (hq, hq), 1) // CHUNK
    tri = col <= row
    sa, sb_lo, sb_hi = _scores(qa, k_lo), _scores(qb, k_lo), _scores(qb, k_hi)
    _, la, acca = _softmax_update(sa, v_lo, ca, tri)
    cb = _softmax_update(sb_lo, v_lo, cb)
    _, lb, accb = _softmax_update(sb_hi, v_hi, cb, tri)
    o_ref[:hq, :] = (acca / la * _silu(g_ref[:hq, :])).astype(o_ref.dtype)
    o_ref[hq:, :] = (accb / lb * _silu(g_ref[hq:, :])).astype(o_ref.dtype)


def mla_attn_prompt(q, k, v, p, *, bsz, seq):
    tq = _pick(seq, (512, 256))
    nq = seq // tq
    return pl.pallas_call(
        functools.partial(_mla_attn_kernel, tq=tq), name="mla_attn",
        grid=(bsz, MLA_HEADS, nq),
        in_specs=[pl.BlockSpec((tq, MLA_PAD), lambda b, h, i: (b * nq + i, h)),
                  pl.BlockSpec((seq, MLA_PAD), lambda b, h, i: (b, h)),
                  pl.BlockSpec((seq, MLA_V), lambda b, h, i: (b, h)),
                  pl.BlockSpec((tq, MLA_V), lambda b, h, i: (b * nq + i, AB_GA // MLA_V + h))],
        out_specs=pl.BlockSpec((tq, MLA_V), lambda b, h, i: (b * nq + i, h)),
        out_shape=jax.ShapeDtypeStruct((bsz * seq, MLA_HEADS * MLA_V), BF16),
        compiler_params=_cparams("parallel", "parallel", "arbitrary"),
    )(q, k, v, p)


def _mla_attn_step_kernel(q_ref, kc_ref, vc_ref, kn_ref, vn_ref, g_ref, o_ref, *, heads):
    t = q_ref.shape[0]
    init = (jnp.full((t, 1), NEG, F32), jnp.zeros((t, 1), F32), jnp.zeros((t, MLA_V), F32))
    for hd in range(heads):
        ks = slice(hd * MLA_PAD, (hd + 1) * MLA_PAD)
        vs = slice(hd * MLA_V, (hd + 1) * MLA_V)
        q = q_ref[:, ks]
        carry = _softmax_step(q, kc_ref[:, ks], vc_ref[:, vs], init)
        _, l, acc = _softmax_step(q, kn_ref[:, ks], vn_ref[:, vs], carry)
        o_ref[:, vs] = (acc / l * _silu(g_ref[:, vs])).astype(o_ref.dtype)


def mla_attn_step(q, kc, vc, kn, vn, p, *, bsz, t, past):
    heads = 4
    kw, vw = heads * MLA_PAD, heads * MLA_V
    return pl.pallas_call(
        functools.partial(_mla_attn_step_kernel, heads=heads), name="mla_attn_step",
        grid=(bsz, MLA_HEADS // heads),
        in_specs=[pl.BlockSpec((t, kw), lambda b, h: (b, h)),
                  pl.BlockSpec((past, kw), lambda b, h: (b, h)),
                  pl.BlockSpec((past, vw), lambda b, h: (b, h)),
                  pl.BlockSpec((t, kw), lambda b, h: (b, h)),
                  pl.BlockSpec((t, vw), lambda b, h: (b, h)),
                  pl.BlockSpec((t, vw), lambda b, h: (b, AB_GA // vw + h))],
        out_specs=pl.BlockSpec((t, vw), lambda b, h: (b, h)),
        out_shape=jax.ShapeDtypeStruct((bsz * t, MLA_HEADS * MLA_V), BF16),
        compiler_params=_cparams("parallel", "parallel"),
    )(q, kc, vc, kn, vn, p)


def _s5_disc_kernel(lre_ref, lim_ref, ldt_ref, bre_ref, bim_ref, lbre_ref, lbim_ref, bbre_ref, bbim_ref):
    lam_re = jnp.minimum(lre_ref[...], -1e-4)
    lam_im = lim_ref[...]
    dt = jnp.exp(ldt_ref[...])
    mag = jnp.exp(lam_re * dt)
    lb_re = mag * jnp.cos(lam_im * dt)
    lb_im = mag * jnp.sin(lam_im * dt)
    nr, ni = lb_re - 1.0, lb_im
    den = lam_re * lam_re + lam_im * lam_im
    f_re = (nr * lam_re + ni * lam_im) / den
    f_im = (ni * lam_re - nr * lam_im) / den
    b_re, b_im = bre_ref[...], bim_ref[...]
    lbre_ref[...] = lb_re
    lbim_ref[...] = lb_im
    bbre_ref[...] = f_re * b_re - f_im * b_im
    bbim_ref[...] = f_re * b_im + f_im * b_re


def s5_discretize(lam_re, lam_im, log_dt, b_re, b_im):
    g, pdim = lam_re.shape
    bt_re = jnp.swapaxes(b_re, 1, 2)
    bt_im = jnp.swapaxes(b_im, 1, 2)
    s3 = jax.ShapeDtypeStruct((g, 1, pdim), F32)
    sb = jax.ShapeDtypeStruct(bt_re.shape, F32)
    return pl.pallas_call(
        _s5_disc_kernel, name="s5_disc", out_shape=[s3, s3, sb, sb],
    )(lam_re.reshape(g, 1, pdim), lam_im.reshape(g, 1, pdim), log_dt.reshape(g, 1, 1), bt_re, bt_im)


def _s5_kernel(u_ref, h0re_ref, h0im_ref, wbu_ref, lbre_ref, lbim_ref, wcre_ref, wcim_ref, d_ref,
               y_ref, hre_ref, him_ref, ut_ref, sre_ref, sim_ref, *, bsz, tc, nsub, group, unroll):
    sub = 8
    assert bsz == sub and SSM_SUB * SSM_GROUP == LANE
    rows = bsz * tc
    sw = SSM_SUB * SSM_STATE
    uw = SSM_SUB * SSM_GROUP
    tiles = sw // LANE

    @pl.when(pl.program_id(1) == 0)
    def _():
        hre_ref[...] = h0re_ref[...]
        him_ref[...] = h0im_ref[...]

    for b in range(bsz):
        for s in range(nsub):
            ut_ref[s, pl.ds(b, tc, stride=bsz), :] = u_ref[b, :, s * uw:(s + 1) * uw]

    for s in range(nsub):
        bu = jnp.dot(ut_ref[s].astype(BF16), wbu_ref[s], preferred_element_type=F32)
        for k in range(tiles):
            sre_ref[s * tiles + k] = bu[:, k * LANE:(k + 1) * LANE]
            sim_ref[s * tiles + k] = bu[:, sw + k * LANE:sw + (k + 1) * LANE]

    for c0 in range(0, nsub * tiles, group):
        lanes = [slice((c0 + k) * LANE, (c0 + k + 1) * LANE) for k in range(group)]
        ar = [jnp.broadcast_to(lbre_ref[:, ls], (sub, LANE)) for ls in lanes]
        ai = [jnp.broadcast_to(lbim_ref[:, ls], (sub, LANE)) for ls in lanes]

        def step(i, carry, c0=c0, ar=ar, ai=ai):
            rs = pl.ds(pl.multiple_of(i * sub, sub), sub)
            out = []
            for k in range(group):
                hr, hi = carry[2 * k], carry[2 * k + 1]
                xr, xi = sre_ref[c0 + k, rs, :], sim_ref[c0 + k, rs, :]
                nr = ar[k] * hr - ai[k] * hi + xr
                ni = ar[k] * hi + ai[k] * hr + xi
                sre_ref[c0 + k, rs, :] = nr
                sim_ref[c0 + k, rs, :] = ni
                out += [nr, ni]
            return tuple(out)

        init = []
        for ls in lanes:
            init += [hre_ref[:, ls], him_ref[:, ls]]
        fin = lax.fori_loop(0, tc, step, tuple(init), unroll=unroll)
        for k, ls in enumerate(lanes):
            hre_ref[:, ls] = fin[2 * k]
            him_ref[:, ls] = fin[2 * k + 1]

    d = d_ref[...]
    for s in range(nsub):
        hre = jnp.concatenate([sre_ref[s * tiles + k] for k in range(tiles)], axis=-1).astype(BF16)
        him = jnp.concatenate([sim_ref[s * tiles + k] for k in range(tiles)], axis=-1).astype(BF16)
        y = jnp.dot(hre, wcre_ref[s], preferred_element_type=F32)
        y = y + jnp.dot(him, wcim_ref[s], preferred_element_type=F32)
        y = y + d[:, s * uw:(s + 1) * uw] * ut_ref[s]
        ut_ref[s] = _gelu_tanh(y)
    for b in range(bsz):
        for s in range(nsub):
            y_ref[b, :, s * uw:(s + 1) * uw] = ut_ref[s, pl.ds(b, tc, stride=bsz), :].astype(BF16)


def _s5_packed_kernel(u_ref, h0re_ref, h0im_ref, wbu_ref, lbre_ref, lbim_ref, wc_ref, d_ref,
                      y_ref, hre_ref, him_ref, ua_ref, st_ref, yt_ref, *, tc, nsub, group, unroll):
    bsz, sub = 4, 8
    rows = sub * tc
    sw = SSM_SUB * SSM_STATE
    uw = SSM_SUB * SSM_GROUP
    tiles = sw // LANE

    @pl.when(pl.program_id(1) == 0)
    def _():
        hre_ref[...] = h0re_ref[...]
        him_ref[...] = h0im_ref[...]
        ua_ref[...] = jnp.zeros(ua_ref.shape, F32)

    for b in range(bsz):
        for s in range(nsub):
            ub = u_ref[b, :, s * uw:(s + 1) * uw]
            ua_ref[s, 0, pl.ds(b, tc, stride=sub), :] = ub
            ua_ref[s, 1, pl.ds(bsz + b, tc, stride=sub), :] = ub

    for s in range(nsub):
        lhs = jnp.concatenate([ua_ref[s, 0], ua_ref[s, 1]], axis=-1).astype(BF16)
        bu = jnp.dot(lhs, wbu_ref[s], preferred_element_type=F32)
        for k in range(tiles):
            st_ref[s * tiles + k] = bu[:, k * LANE:(k + 1) * LANE]

    low = lax.broadcasted_iota(jnp.int32, (sub, LANE), 0) < bsz
    for c0 in range(0, nsub * tiles, group):
        lanes = [slice((c0 + k) * LANE, (c0 + k + 1) * LANE) for k in range(group)]
        a1 = [jnp.broadcast_to(lbre_ref[:, ls], (sub, LANE)) for ls in lanes]
        a2 = [jnp.where(low, -1.0, 1.0) * jnp.broadcast_to(lbim_ref[:, ls], (sub, LANE)) for ls in lanes]

        def step(t, carry, c0=c0, a1=a1, a2=a2):
            rs = pl.ds(pl.multiple_of(t * sub, sub), sub)
            out = []
            for k in range(group):
                h = carry[k]
                n = a1[k] * h + a2[k] * pltpu.roll(h, bsz, 0) + st_ref[c0 + k, rs, :]
                st_ref[c0 + k, rs, :] = n
                out.append(n)
            return tuple(out)

        init = tuple(jnp.concatenate([hre_ref[:, ls], him_ref[:, ls]], axis=0) for ls in lanes)
        fin = lax.fori_loop(0, tc, step, init, unroll=unroll)
        for k, ls in enumerate(lanes):
            hre_ref[:, ls] = fin[k][:bsz]
            him_ref[:, ls] = fin[k][bsz:]

    is_re = lax.broadcasted_iota(jnp.int32, (rows, uw), 0) % sub < bsz
    for s in range(nsub):
        hcat = jnp.concatenate([st_ref[s * tiles + k] for k in range(tiles)], axis=-1).astype(BF16)
        z = jnp.dot(hcat, wc_ref[s], preferred_element_type=F32)
        yt_ref[s] = jnp.where(is_re, z[:, :uw], z[:, uw:])
    d = d_ref[...]
    for b in range(bsz):
        for s in range(nsub):
            ls = slice(s * uw, (s + 1) * uw)
            y = yt_ref[s, pl.ds(b, tc, stride=sub), :] + yt_ref[s, pl.ds(bsz + b, tc, stride=sub), :]
            y_ref[b, :, ls] = _gelu_tanh(y + d[:, ls] * u_ref[b, :, ls]).astype(BF16)


def _s5_packed(p3, h0_re, h0_im, wbu, lb_re, lb_im, wc_re, wc_im, d_skip, *, tc):
    bsz, t, _ = p3.shape
    nsub = 4
    uw = nsub * SSM_SUB * SSM_GROUP
    sw = nsub * SSM_SUB * SSM_STATE
    nblk = wbu.shape[0] // nsub
    half = wbu.shape[2] // 2
    wbu2 = jnp.concatenate([wbu[:, :, :half], wbu[:, :, half:]], axis=1)
    wc2 = jnp.concatenate([wc_re, wc_im], axis=2)
    kern = functools.partial(_s5_packed_kernel, tc=tc, nsub=nsub, group=8, unroll=4)
    st_spec = pl.BlockSpec((bsz, sw), lambda g, c: (0, g))
    rows = 8 * tc
    return pl.pallas_call(
        kern, name="s5_mixer",
        grid=(nblk, t // tc),
        in_specs=[pl.BlockSpec((bsz, tc, uw), lambda g, c: (0, c, AB_U // uw + g)),
                  st_spec, st_spec,
                  pl.BlockSpec((nsub,) + wbu2.shape[1:], lambda g, c: (g, 0, 0)),
                  pl.BlockSpec((1, sw), lambda g, c: (0, g)),
                  pl.BlockSpec((1, sw), lambda g, c: (0, g)),
                  pl.BlockSpec((nsub,) + wc2.shape[1:], lambda g, c: (g, 0, 0)),
                  pl.BlockSpec((1, uw), lambda g, c: (0, g))],
        out_specs=[pl.BlockSpec((bsz, tc, uw), lambda g, c: (0, c, g)), st_spec, st_spec],
        out_shape=[jax.ShapeDtypeStruct((bsz, t, uw * nblk), BF16),
                   jax.ShapeDtypeStruct(h0_re.shape, F32),
                   jax.ShapeDtypeStruct(h0_im.shape, F32)],
        scratch_shapes=[pltpu.VMEM((nsub, 2, rows, LANE), F32),
                        pltpu.VMEM((sw // LANE, rows, LANE), F32),
                        pltpu.VMEM((nsub, rows, LANE), F32)],
        compiler_params=_cparams("parallel", "arbitrary"),
    )(p3, h0_re, h0_im, wbu2, lb_re, lb_im, wc2, d_skip)


def s5_mixer(p3, h0_re, h0_im, wbu, lb_re, lb_im, wc_re, wc_im, d_skip, *, tc):
    bsz, t, _ = p3.shape
    if bsz == 4:
        return _s5_packed(p3, h0_re, h0_im, wbu, lb_re, lb_im, wc_re, wc_im, d_skip, tc=tc)
    nsub = 4
    uw = nsub * SSM_SUB * SSM_GROUP
    sw = nsub * SSM_SUB * SSM_STATE
    nblk = wbu.shape[0] // nsub
    bw = uw * nblk
    kern = functools.partial(_s5_kernel, bsz=bsz, tc=tc, nsub=nsub, group=4, unroll=4)
    st_spec = pl.BlockSpec((bsz, sw), lambda g, c: (0, g))
    return pl.pallas_call(
        kern, name="s5_mixer",
        grid=(nblk, t // tc),
        in_specs=[pl.BlockSpec((bsz, tc, uw), lambda g, c: (0, c, AB_U // uw + g)),
                  st_spec, st_spec,
                  pl.BlockSpec((nsub,) + wbu.shape[1:], lambda g, c: (g, 0, 0)),
                  pl.BlockSpec((1, sw), lambda g, c: (0, g)),
                  pl.BlockSpec((1, sw), lambda g, c: (0, g)),
                  pl.BlockSpec((nsub,) + wc_re.shape[1:], lambda g, c: (g, 0, 0)),
                  pl.BlockSpec((nsub,) + wc_im.shape[1:], lambda g, c: (g, 0, 0)),
                  pl.BlockSpec((1, uw), lambda g, c: (0, g))],
        out_specs=[pl.BlockSpec((bsz, tc, uw), lambda g, c: (0, c, g)), st_spec, st_spec],
        out_shape=[jax.ShapeDtypeStruct((bsz, t, bw), BF16),
                   jax.ShapeDtypeStruct(h0_re.shape, F32),
                   jax.ShapeDtypeStruct(h0_im.shape, F32)],
        scratch_shapes=[pltpu.VMEM((nsub, bsz * tc, LANE), F32),
                        pltpu.VMEM((sw // LANE, bsz * tc, LANE), F32),
                        pltpu.VMEM((sw // LANE, bsz * tc, LANE), F32)],
        compiler_params=_cparams("parallel", "arbitrary"),
    )(p3, h0_re, h0_im, wbu, lb_re, lb_im, wc_re, wc_im, d_skip)


def _glu_kernel(y_ref, wa32_ref, wb32_ref, g_ref, o_ref, wa_ref, wb_ref):
    @pl.when(pl.program_id(1) == 0)
    def _():
        wa_ref[...] = wa32_ref[...].astype(BF16)
        wb_ref[...] = wb32_ref[...].astype(BF16)

    y = y_ref[...]
    ga = jnp.dot(y, wa_ref[...], preferred_element_type=F32)
    gb = jnp.dot(y, wb_ref[...], preferred_element_type=F32)
    o_ref[...] = (ga * jax.nn.sigmoid(gb) * _silu(g_ref[...])).astype(o_ref.dtype)


def glu_gate(y, w_glu, layer, p, *, tn=512):
    m, k = y.shape
    n = w_glu.shape[2] // 2
    tm = _pick(m, (1024, 512, 256, 128))
    nj = n // tn
    return pl.pallas_call(
        _glu_kernel, name="glu_gate",
        grid=(nj, m // tm),
        in_specs=[pl.BlockSpec((tm, k), lambda j, i: (i, 0)),
                  pl.BlockSpec((None, k, tn), lambda j, i: (layer, 0, j)),
                  pl.BlockSpec((None, k, tn), lambda j, i: (layer, 0, nj + j)),
                  pl.BlockSpec((tm, tn), lambda j, i: (i, AB_GB // tn + j))],
        out_specs=pl.BlockSpec((tm, tn), lambda j, i: (i, j)),
        out_shape=jax.ShapeDtypeStruct((m, n), BF16),
        scratch_shapes=[pltpu.VMEM((k, tn), BF16), pltpu.VMEM((k, tn), BF16)],
        compiler_params=_cparams("arbitrary", "arbitrary"),
    )(y, w_glu, w_glu, p)


C_Q = 0
C_G = 4096
C_K = 8192
C_V = 8704
KVW = SWA_KV_HEADS * SWA_HEAD_DIM


def _k_norm_kernel(k_ref, g_ref, o_ref):
    k = k_ref[...]
    g = g_ref[...]
    for h in range(SWA_KV_HEADS):
        sl = slice(h * SWA_HEAD_DIM, (h + 1) * SWA_HEAD_DIM)
        o_ref[:, sl] = _rms(k[:, sl], g)


def swa_k_norm(pc, gk):
    m = pc.shape[0]
    tm = _pick(m, (512, 256, 128))
    return pl.pallas_call(
        _k_norm_kernel, name="swa_k_norm",
        grid=(m // tm,),
        in_specs=[pl.BlockSpec((tm, KVW), lambda i: (i, C_K // KVW)),
                  pl.BlockSpec((1, SWA_HEAD_DIM), lambda i: (0, 0))],
        out_specs=pl.BlockSpec((tm, KVW), lambda i: (i, 0)),
        out_shape=jax.ShapeDtypeStruct((m, KVW), F32),
        compiler_params=_cparams("parallel"),
    )(pc, gk.reshape(1, -1))


def _swa_prompt_kernel(sink_ref, q_ref, kp_ref, kc_ref, vp_ref, vc_ref, g_ref, gq_ref, b_ref, o_ref, *, pair_batch):
    nk = b_ref.shape[-1]
    pad = nk - kp_ref.shape[0] - kc_ref.shape[0]
    fill = [jnp.zeros((pad, KVW), F32)] if pad else []
    kcat = jnp.concatenate([kp_ref[...], kc_ref[...]] + fill, axis=0)
    vcat = jnp.concatenate([vp_ref[...], vc_ref[...]] + fill, axis=0)
    seg_r = lax.broadcasted_iota(jnp.int32, (LANE, LANE), 0) // SWA_HEAD_DIM
    seg_c = lax.broadcasted_iota(jnp.int32, (LANE, LANE), 1) // SWA_HEAD_DIM
    seg_ones = (seg_r == seg_c).astype(BF16)
    left = lax.broadcasted_iota(jnp.int32, (1, LANE), 1) < SWA_HEAD_DIM
    gq = gq_ref[...] * (SWA_HEAD_DIM ** -0.5 * LOG2E)
    nt = (((1,), (1,)), ((), ()))

    def halves(slab, half):
        own = jnp.where(left if half == 0 else jnp.logical_not(left), slab, 0.0)
        other = pltpu.roll(own, SWA_HEAD_DIM, 1)
        lo, hi = (own, other) if half == 0 else (other, own)
        return lo.astype(BF16), hi.astype(BF16)

    kv = {}

    def kv_halves(kvh):
        if kvh not in kv:
            ks = slice((kvh // 2) * LANE, (kvh // 2 + 1) * LANE)
            kv.clear()
            kv[kvh] = halves(kcat[:, ks], kvh % 2) + halves(vcat[:, ks], kvh % 2)
        return kv[kvh]

    def scores(pair):
        sl = slice(pair * LANE, (pair + 1) * LANE)
        k_lo, k_hi, _, _ = kv_halves(pair // (SWA_GQ // 2))
        q = q_ref[:, sl]
        x2 = q * q
        x2_hi = x2.astype(BF16)
        x2_lo = (x2 - x2_hi.astype(F32)).astype(BF16)
        ss = (jnp.dot(x2_hi, seg_ones, preferred_element_type=F32)
              + jnp.dot(x2_lo, seg_ones, preferred_element_type=F32))
        qn = (q * lax.rsqrt(ss * (1.0 / SWA_HEAD_DIM) + EPS) * gq).astype(BF16)
        return [lax.dot_general(qn, k_lo, nt, preferred_element_type=F32) + b_ref[2 * pair],
                lax.dot_general(qn, k_hi, nt, preferred_element_type=F32) + b_ref[2 * pair + 1]]

    def finish(pairs, ss, vhs):
        sinks = [sink_ref[2 * p + j] * LOG2E for p in pairs for j in range(2)]
        flat = [s for pr in ss for s in pr]
        ms = [jnp.maximum(jnp.max(s, axis=-1, keepdims=True), sk) for s, sk in zip(flat, sinks)]
        es = [jnp.exp2(s - m) for s, m in zip(flat, ms)]
        ds = [jnp.sum(e, axis=-1, keepdims=True) + jnp.exp2(sk - m) for e, m, sk in zip(es, ms, sinks)]
        for i, p in enumerate(pairs):
            v_lo, v_hi = vhs[i]
            o = (jnp.dot(es[2 * i].astype(BF16), v_lo, preferred_element_type=F32)
                 + jnp.dot(es[2 * i + 1].astype(BF16), v_hi, preferred_element_type=F32))
            inv = jnp.where(left, 1.0 / ds[2 * i], 1.0 / ds[2 * i + 1])
            sl = slice(p * LANE, (p + 1) * LANE)
            o_ref[:, sl] = (o * inv * _silu(g_ref[:, sl])).astype(o_ref.dtype)

    npairs = SWA_HEADS // 2
    batches = [list(range(b, b + pair_batch)) for b in range(0, npairs, pair_batch)]
    nxt = ([scores(p) for p in batches[0]], [kv_halves(p // (SWA_GQ // 2))[2:] for p in batches[0]])
    for bi, pairs in enumerate(batches):
        cur = nxt
        if bi + 1 < len(batches):
            nb = batches[bi + 1]
            nxt = ([scores(p) for p in nb], [kv_halves(p // (SWA_GQ // 2))[2:] for p in nb])
        finish(pairs, *cur)


def swa_attn_prompt(pc, kn, gq, sinks, bias, *, bsz, seq):
    tq = WINDOW
    nq = seq // tq
    dm = SWA_HEADS * SWA_HEAD_DIM
    prev = lambda b, i: (b * nq + jnp.maximum(i - 1, 0), 0)
    cur = lambda b, i: (b * nq + i, 0)
    return pl.pallas_call(
        functools.partial(_swa_prompt_kernel, pair_batch=2), name="swa_attn",
        grid=(bsz, nq),
        in_specs=[pl.BlockSpec(memory_space=pltpu.SMEM),
                  pl.BlockSpec((tq, dm), lambda b, i: (b * nq + i, C_Q // dm)),
                  pl.BlockSpec((tq, KVW), prev), pl.BlockSpec((tq, KVW), cur),
                  pl.BlockSpec((tq, KVW), lambda b, i: (b * nq + jnp.maximum(i - 1, 0), C_V // KVW)),
                  pl.BlockSpec((tq, KVW), lambda b, i: (b * nq + i, C_V // KVW)),
                  pl.BlockSpec((tq, dm), lambda b, i: (b * nq + i, C_G // dm)),
                  pl.BlockSpec((1, LANE), lambda b, i: (0, 0)),
                  pl.BlockSpec((None,) + bias.shape[1:], lambda b, i: (jnp.minimum(i, 1), 0, 0, 0))],
        out_specs=pl.BlockSpec((tq, dm), cur),
        out_shape=jax.ShapeDtypeStruct((bsz * seq, dm), BF16),
        compiler_params=_cparams("parallel", "arbitrary"),
    )(sinks, pc, kn, kn, pc, pc, pc, jnp.tile(gq.reshape(1, -1), (1, 2)), bias)


def swa_attn_step(pc, k_cache, kn, v_cache, layer, gq, sinks, bias, *, bsz, t):
    dm = SWA_HEADS * SWA_HEAD_DIM
    row = lambda b: (b, 0)
    cache = pl.BlockSpec((None, WINDOW, KVW), lambda b: (layer, b, 0))
    return pl.pallas_call(
        functools.partial(_swa_prompt_kernel, pair_batch=4), name="swa_attn_step",
        grid=(bsz,),
        in_specs=[pl.BlockSpec(memory_space=pltpu.SMEM),
                  pl.BlockSpec((t, dm), lambda b: (b, C_Q // dm)),
                  cache, pl.BlockSpec((t, KVW), row), cache,
                  pl.BlockSpec((t, KVW), lambda b: (b, C_V // KVW)),
                  pl.BlockSpec((t, dm), lambda b: (b, C_G // dm)),
                  pl.BlockSpec((1, LANE), lambda b: (0, 0)),
                  pl.BlockSpec(bias.shape, lambda b: (0, 0, 0))],
        out_specs=pl.BlockSpec((t, dm), row),
        out_shape=jax.ShapeDtypeStruct((bsz * t, dm), BF16),
        compiler_params=_cparams("parallel"),
    )(sinks, pc, k_cache, kn, v_cache, pc, pc, jnp.tile(gq.reshape(1, -1), (1, 2)), bias)


def _rope_tables(pos, reps=1):
    half = MLA_ROPE // 2
    inv = ROPE_THETA ** (-jnp.arange(half, dtype=F32) / half)
    ang = pos.astype(F32)[:, None] * inv
    cos, sin = jnp.cos(ang), jnp.sin(ang)
    z = jnp.zeros_like(cos)
    tabs = (jnp.concatenate([cos, cos, z, z], -1), jnp.concatenate([-sin, z, z, z], -1),
            jnp.concatenate([z, sin, z, z], -1))
    return tuple(jnp.tile(t, (reps, 1)) for t in tabs)


def _rel_bucket(rel):
    nb = REL_BUCKETS // 2
    max_exact = nb // 2
    ret = jnp.where(rel > 0, nb, 0)
    n = jnp.abs(rel)
    nf = jnp.maximum(n, 1).astype(F32)
    large = max_exact + (jnp.log(nf / max_exact) / math.log(REL_MAX_DIST / max_exact)
                         * (nb - max_exact)).astype(jnp.int32)
    large = jnp.minimum(large, nb - 1)
    return ret + jnp.where(n < max_exact, n, large)


def _rel_bias(table, rel, mask):
    onehot = (_rel_bucket(rel)[:, :, None] == jnp.arange(REL_BUCKETS)).astype(F32)
    b = jnp.einsum("qkb,bh->hqk", onehot, table.astype(F32) * LOG2E, precision=lax.Precision.HIGHEST)
    return jnp.where(mask[None], b, NEG)


def _w_in_relayout_kernel(w_ref, o_ref):
    rows = w_ref.shape[0]
    left = lax.broadcasted_iota(jnp.int32, (1, LANE), 1) < MLA_ROPE
    o_ref[:, AB_CQ:AB_CQ + 768] = w_ref[:, 0:768].astype(BF16)
    kr = w_ref[:, 1280:1280 + LANE]
    o_ref[:, AB_KR:AB_KR + LANE] = jnp.where(left, kr, 0.0).astype(BF16)
    o_ref[:, AB_KR + LANE:AB_CKV] = jnp.zeros((rows, AB_CKV - AB_KR - LANE), BF16)
    o_ref[:, AB_CKV:AB_CKV + 512] = w_ref[:, 768:1280].astype(BF16)
    src = 1280 + MLA_ROPE
    for c in range(AB_GA, AB_COLS, 512):
        o_ref[:, c:c + 512] = w_ref[:, src + c - AB_GA:src + c - AB_GA + 512].astype(BF16)


def w_in_relayout(w_in):
    nl, d, n = w_in.shape
    assert n == 1280 + MLA_ROPE + (AB_COLS - AB_GA)
    tk = 256
    return pl.pallas_call(
        _w_in_relayout_kernel, name="w_in_relayout",
        grid=(nl, d // tk),
        in_specs=[pl.BlockSpec((None, tk, n), lambda l, i: (l, i, 0))],
        out_specs=pl.BlockSpec((None, tk, AB_COLS), lambda l, i: (l, i, 0)),
        out_shape=jax.ShapeDtypeStruct((nl, d, AB_COLS), BF16),
        compiler_params=_cparams("parallel", "parallel"),
    )(w_in)


def _prep_ab_weights(w_in, w_uq, w_ukv):
    nl, d, _ = w_in.shape
    w_in_r = w_in_relayout(w_in)
    uq = w_uq.astype(BF16).reshape(nl, w_uq.shape[1], MLA_HEADS, MLA_QK)
    uq = jnp.pad(uq, ((0, 0), (0, 0), (0, 0), (0, MLA_PAD - MLA_QK))).reshape(nl, w_uq.shape[1], -1)
    ukv = w_ukv.astype(BF16).reshape(nl, w_ukv.shape[1], MLA_HEADS, MLA_NOPE + MLA_V)
    wn = ukv[:, :, :, :MLA_NOPE].reshape(nl, w_ukv.shape[1], -1)
    wv = ukv[:, :, :, MLA_NOPE:].reshape(nl, w_ukv.shape[1], -1)
    return dict(w_in=w_in_r, w_uq=uq, wn=wn, wv=wv)


def _head_gains(g):
    return g[:MLA_NOPE].reshape(1, LANE), jnp.pad(g[MLA_NOPE:], (0, LANE - MLA_ROPE)).reshape(1, LANE)


def _prep_s5(lam_re, lam_im, log_dt, b_re, b_im, c_re, c_im):
    g = lam_re.shape[0]
    lb_re, lb_im, bbt_re, bbt_im = s5_discretize(lam_re, lam_im, log_dt, b_re, b_im)
    eye = jnp.eye(SSM_SUB, dtype=F32)
    ns = g // SSM_SUB
    bb = jnp.stack([bbt_re, bbt_im]).reshape(2, ns, SSM_SUB, SSM_GROUP, SSM_STATE)
    wbu = bb[:, :, :, :, None, :] * eye[None, None, :, None, :, None]
    wbu = jnp.transpose(wbu, (1, 2, 3, 0, 4, 5)).reshape(ns, SSM_SUB * SSM_GROUP, 2 * SSM_SUB * SSM_STATE)

    def readout(c):
        c4 = c.astype(F32).reshape(ns, SSM_SUB, SSM_GROUP, SSM_STATE)
        w = c4[:, :, :, None, :] * eye[None, :, None, :, None]
        return jnp.transpose(w, (0, 3, 4, 1, 2)).reshape(ns, SSM_SUB * SSM_STATE, SSM_SUB * SSM_GROUP)

    return dict(wbu=wbu.astype(BF16), lb_re=lb_re.reshape(1, -1), lb_im=lb_im.reshape(1, -1),
                wc_re=readout(c_re).astype(BF16), wc_im=readout(-c_im).astype(BF16))


def _ab_layer(x, i, w, prm, s5w, tabs, *, bsz, seq, past):
    m = x.shape[0]
    gqn, gqr = _head_gains(prm["q_norm"][i])
    gkn, gkr = _head_gains(prm["k_norm"][i])
    p = norm_matmul(x, prm["norm"][i], w["w_in"], i, tn=768)
    q = mla_q_prep(p, prm["q_lora_norm"][i], w["w_uq"], i, tabs, gqn, gqr)
    lat, krp, kro = mla_latent(p, prm["kv_lora_norm"][i], tabs)
    k, v = mla_kv_expand(lat[None], krp[None], 0, w["wn"], w["wv"], i, gkn, gkr)
    p3 = p.reshape(bsz, seq, AB_COLS)
    gdim = s5w["lb_re"].shape[1]
    d_skip = prm["d_skip"][i].reshape(1, -1)
    s5args = (s5w["wbu"], s5w["lb_re"], s5w["lb_im"], s5w["wc_re"], s5w["wc_im"], d_skip)
    if past is None:
        a_out = mla_attn_prompt(q, k, v, p, bsz=bsz, seq=seq)
        h0 = jnp.zeros((bsz, gdim), F32)
        y, h_re, h_im = s5_mixer(p3, h0, h0, *s5args, tc=2 * CHUNK)
    else:
        lat_c, kr_c, h0_re, h0_im = past
        kc, vc = mla_kv_expand(lat_c, kr_c, i, w["wn"], w["wv"], i, gkn, gkr)
        a_out = mla_attn_step(q, kc, vc, k, v, p, bsz=bsz, t=seq, past=lat_c.shape[1] // bsz)
        y, h_re, h_im = s5_mixer(p3, h0_re.reshape(bsz, gdim), h0_im.reshape(bsz, gdim), *s5args, tc=seq)
    b_out = glu_gate(y.reshape(m, -1), prm["w_glu"], i, p)
    x_new = out_proj([a_out, b_out], prm["w_out"], i, x)
    return x_new, (lat, kro, h_re, h_im)


def _c_in_col(j):
    return jnp.where(j < 8, j, jnp.where(j < 10, j + 8, j - 2))


def _c_layer(x, i, prm, biases, *, bsz, seq, past):
    assert prm["w_in"].shape[2] == C_V + KVW and C_G == 8 * KVW and C_K == 16 * KVW
    pc = wcast_matmul(rms_cast(x, prm["norm"][i]), prm["w_in"], i, _c_in_col, tn=KVW)
    gq, gk, sinks = prm["q_norm"][i], prm["k_norm"][i], prm["sinks"][i]
    kn = swa_k_norm(pc, gk)
    v = pc[:, C_V:C_V + KVW]
    if past is None:
        a = swa_attn_prompt(pc, kn, gq, sinks, *biases, bsz=bsz, seq=seq)
        k_buf = kn.reshape(bsz, seq, KVW)[:, -WINDOW:]
        v_buf = v.reshape(bsz, seq, KVW)[:, -WINDOW:]
    else:
        k_c, v_c = past
        a = swa_attn_step(pc, k_c, kn, v_c, i, gq, sinks, *biases, bsz=bsz, t=seq)
        k_buf = jnp.concatenate([k_c[i].reshape(bsz, WINDOW, KVW), kn.reshape(bsz, seq, KVW)], axis=1)[:, -WINDOW:]
        v_buf = jnp.concatenate([v_c[i].reshape(bsz, WINDOW, KVW), v.reshape(bsz, seq, KVW)], axis=1)[:, -WINDOW:]
    x_new = out_proj([a], prm["w_out"], i, x)
    shp = (bsz, WINDOW, SWA_KV_HEADS, SWA_HEAD_DIM)
    return x_new, (k_buf.reshape(shp), v_buf.reshape(shp))


def kernel(x_prompt, x_sample, cache_mla_latent, cache_mla_krope, state_ssm_re, state_ssm_im, cache_swa_k, cache_swa_v, rel_bias_table, ab_norm, ab_w_in, ab_q_lora_norm, ab_kv_lora_norm, ab_w_uq, ab_w_ukv, ab_q_norm, ab_k_norm, ssm_lambda_re, ssm_lambda_im, ssm_b_re, ssm_b_im, ssm_c_re, ssm_c_im, ssm_log_dt, ssm_d, ssm_w_glu, ab_w_out, c_norm, c_w_in, c_q_norm, c_k_norm, c_sinks, c_w_out):
    bp, lp, d = x_prompt.shape
    bs, ts, _ = x_sample.shape
    past_len = cache_mla_latent.shape[2]
    depth = ab_norm.shape[0] + c_norm.shape[0]
    assert lp % 256 == 0 and (bs * ts) % 128 == 0 and past_len % 512 == 0 and d == SWA_HEADS * SWA_HEAD_DIM

    q_pos = past_len + np.arange(ts)
    assert np.all((np.arange(past_len + ts)[None, :] // CHUNK) <= (q_pos[:, None] // CHUNK))
    k_pos = past_len - WINDOW + np.arange(WINDOW + ts)
    qch, kch = q_pos[:, None] // CHUNK, k_pos[None, :] // CHUNK
    assert np.all((kch <= qch) & (kch >= qch - WINDOW // CHUNK))

    tabs_p = _rope_tables(jnp.arange(lp))
    tabs_s = _rope_tables(past_len + jnp.arange(ts), reps=bs)

    qi = np.arange(WINDOW)[:, None]
    kj = np.arange(2 * WINDOW)[None, :]
    band = (kj // CHUNK - 2 <= qi // CHUNK) & (kj // CHUNK >= qi // CHUNK)
    bias_p = _rel_bias(rel_bias_table, jnp.asarray(kj - WINDOW - qi), jnp.asarray(band))
    biases_p = (jnp.stack([jnp.where(jnp.asarray(kj < WINDOW)[None], NEG, bias_p), bias_p]),)
    kj_s = np.arange(2 * WINDOW)[None, :]
    rel_s = np.where(kj_s < WINDOW + ts, (past_len - WINDOW + kj_s) - q_pos[:, None], 0)
    biases_s = (_rel_bias(rel_bias_table, jnp.asarray(rel_s),
                          jnp.asarray(np.broadcast_to(kj_s < WINDOW + ts, rel_s.shape))),)

    xp = x_prompt.reshape(bp * lp, d)
    xs = x_sample.reshape(bs * ts, d)
    outs_p = [[] for _ in range(6)]
    outs_s = [[] for _ in range(6)]
    n_ab = ab_norm.shape[0]
    ab_w = _prep_ab_weights(ab_w_in, ab_w_uq, ab_w_ukv)
    ab_prm = dict(norm=ab_norm, q_lora_norm=ab_q_lora_norm, kv_lora_norm=ab_kv_lora_norm, q_norm=ab_q_norm,
                  k_norm=ab_k_norm, d_skip=ssm_d, w_glu=ssm_w_glu, w_out=ab_w_out)
    c_prm = dict(norm=c_norm, w_in=c_w_in, q_norm=c_q_norm, k_norm=c_k_norm, sinks=c_sinks, w_out=c_w_out)
    lat_cache = cache_mla_latent.reshape(n_ab, bs * past_len, -1)
    kr_cache = jnp.pad(cache_mla_krope.reshape(n_ab, bs * past_len, -1), ((0, 0), (0, 0), (0, LANE - MLA_ROPE)))
    swa_past = (cache_swa_k.reshape(-1, bs * WINDOW, KVW), cache_swa_v.reshape(-1, bs * WINDOW, KVW))
    for layer in range(depth):
        i = layer // 2
        if layer % 2 == 0:
            s5w = _prep_s5(ssm_lambda_re[i], ssm_lambda_im[i], ssm_log_dt[i], ssm_b_re[i], ssm_b_im[i],
                           ssm_c_re[i], ssm_c_im[i])
            xp, (a, b, c, e) = _ab_layer(xp, i, ab_w, ab_prm, s5w, tabs_p, bsz=bp, seq=lp, past=None)
            outs_p[0].append(a.reshape(bp, lp, -1)); outs_p[1].append(b.reshape(bp, lp, -1))
            outs_p[2].append(c.reshape(bp, -1, SSM_STATE)); outs_p[3].append(e.reshape(bp, -1, SSM_STATE))
            past = (lat_cache, kr_cache, state_ssm_re[i], state_ssm_im[i])
            xs, (a, b, c, e) = _ab_layer(xs, i, ab_w, ab_prm, s5w, tabs_s, bsz=bs, seq=ts, past=past)
            outs_s[0].append(a.reshape(bs, ts, -1)); outs_s[1].append(b.reshape(bs, ts, -1))
            outs_s[2].append(c.reshape(bs, -1, SSM_STATE)); outs_s[3].append(e.reshape(bs, -1, SSM_STATE))
        else:
            xp, (a, b) = _c_layer(xp, i, c_prm, biases_p, bsz=bp, seq=lp, past=None)
            outs_p[4].append(a); outs_p[5].append(b)
            xs, (a, b) = _c_layer(xs, i, c_prm, biases_s, bsz=bs, seq=ts, past=swa_past)
            outs_s[4].append(a); outs_s[5].append(b)
    return (xp.reshape(bp, lp, d), xs.reshape(bs, ts, d),
            *[jnp.stack(o) for o in outs_p], *[jnp.stack(o) for o in outs_s])
```

```python
import functools
import math

import jax
import jax.numpy as jnp
import numpy as np
from jax import lax
from jax.experimental import pallas as pl
from jax.experimental.pallas import tpu as pltpu

F32 = jnp.float32
BF16 = jnp.bfloat16

EPS = 1e-6
CHUNK = 64
ROPE_THETA = 10000.0
MLA_HEADS = 16
MLA_NOPE = 128
MLA_ROPE = 64
MLA_QK = MLA_NOPE + MLA_ROPE
MLA_V = 128
MLA_PAD = 256
SSM_GROUP = 16
SSM_STATE = 64
SSM_SUB = 8
SWA_HEADS = 64
SWA_KV_HEADS = 8
SWA_GQ = SWA_HEADS // SWA_KV_HEADS
SWA_HEAD_DIM = 64
WINDOW = 128
REL_BUCKETS = 32
REL_MAX_DIST = 128
NEG = -1e30
LOG2E = math.log2(math.e)
LANE = 128
VMEM_LIMIT = 56 * 1024 * 1024

AB_CQ = 0
AB_CKV = 768
AB_KR = 1280
AB_SRC_BIG = 1344
AB_GA = 1536
AB_U = 3584
AB_GB = 5632
AB_COLS = 7680


def _cparams(*sem):
    return pltpu.CompilerParams(dimension_semantics=sem, vmem_limit_bytes=VMEM_LIMIT)


def _rms(x, g):
    return x * lax.rsqrt(jnp.mean(x * x, axis=-1, keepdims=True) + EPS) * g


def _silu(x):
    return x * jax.nn.sigmoid(x)


def _gelu_tanh(x):
    return 0.5 * x * (1.0 + jnp.tanh(math.sqrt(2.0 / math.pi) * (x + 0.044715 * (x * x * x))))


def _pick(n, prefs):
    for p in prefs:
        if n % p == 0:
            return p
    raise ValueError(f"no tile for {n} in {prefs}")


def _rms_cast_kernel(x_ref, g_ref, o_ref):
    o_ref[...] = _rms(x_ref[...], g_ref[...]).astype(o_ref.dtype)


def rms_cast(x, g):
    m, k = x.shape
    tm = _pick(m, (512, 256, 128))
    return pl.pallas_call(
        _rms_cast_kernel, name="rms_cast",
        grid=(m // tm,),
        in_specs=[pl.BlockSpec((tm, k), lambda i: (i, 0)), pl.BlockSpec((1, k), lambda i: (0, 0))],
        out_specs=pl.BlockSpec((tm, k), lambda i: (i, 0)),
        out_shape=jax.ShapeDtypeStruct((m, k), BF16),
        compiler_params=_cparams("parallel"),
    )(x, g.reshape(1, k))


def _wcast_mm_kernel(h_ref, w_ref, o_ref, wb_ref):
    @pl.when(pl.program_id(1) == 0)
    def _():
        wb_ref[...] = w_ref[...].astype(BF16)

    o_ref[...] = jnp.dot(h_ref[...], wb_ref[...], preferred_element_type=F32).astype(o_ref.dtype)


def wcast_matmul(h, w, layer, out_col, *, tn):
    m, k = h.shape
    n = w.shape[2]
    tm = _pick(m, (1024, 512, 256, 128))
    return pl.pallas_call(
        _wcast_mm_kernel, name="wcast_mm",
        grid=(n // tn, m // tm),
        in_specs=[pl.BlockSpec((tm, k), lambda j, i: (i, 0)),
                  pl.BlockSpec((None, k, tn), lambda j, i: (layer, 0, j))],
        out_specs=pl.BlockSpec((tm, tn), lambda j, i: (i, out_col(j))),
        out_shape=jax.ShapeDtypeStruct((m, n), F32),
        scratch_shapes=[pltpu.VMEM((k, tn), BF16)],
        compiler_params=_cparams("arbitrary", "arbitrary"),
    )(h, w)


def _wcast_mm_nt_kernel(h_ref, wt_ref, o_ref, wb_ref):
    @pl.when(pl.program_id(1) == 0)
    def _():
        wb_ref[...] = wt_ref[...].astype(BF16)

    o_ref[...] = lax.dot_general(h_ref[...], wb_ref[...], (((1,), (1,)), ((), ())),
                                 preferred_element_type=F32)


def ab_in_proj(h, w_t, layer):
    m, k = h.shape
    tn = 512
    assert w_t.shape[1] == AB_SRC_BIG + AB_COLS - AB_GA and AB_GA == 3 * tn
    tm = _pick(m, (1024, 512, 256, 128))
    src = lambda j: pl.multiple_of(
        jnp.where(j < AB_GA // tn, j * tn, AB_SRC_BIG + (j - AB_GA // tn) * tn), MLA_ROPE)
    return pl.pallas_call(
        _wcast_mm_nt_kernel, name="ab_in_proj",
        grid=(AB_COLS // tn, m // tm),
        in_specs=[pl.BlockSpec((tm, k), lambda j, i: (i, 0)),
                  pl.BlockSpec((None, pl.Element(tn), pl.Element(k)), lambda j, i: (layer, src(j), 0))],
        out_specs=pl.BlockSpec((tm, tn), lambda j, i: (i, j)),
        out_shape=jax.ShapeDtypeStruct((m, AB_COLS), F32),
        scratch_shapes=[pltpu.VMEM((tn, k), BF16)],
        compiler_params=_cparams("arbitrary", "arbitrary"),
    )(h, w_t)


def _out_proj_kernel(*refs, n_lhs):
    lhs = refs[:n_lhs]
    w_ref, x_ref, o_ref, wb_ref = refs[n_lhs:]

    @pl.when(pl.program_id(1) == 0)
    def _():
        wb_ref[...] = w_ref[...].astype(BF16)

    acc = x_ref[...]
    off = 0
    for a in lhs:
        kk = a.shape[1]
        acc = acc + jnp.dot(a[...], wb_ref[off:off + kk, :], preferred_element_type=F32)
        off += kk
    o_ref[...] = acc


def out_proj(lhs, w, layer, x, *, tn=512):
    m, n = x.shape
    k = w.shape[1]
    tm = _pick(m, (1024, 512, 256, 128))
    in_specs = [pl.BlockSpec((tm, a.shape[1]), lambda j, i: (i, 0)) for a in lhs]
    in_specs += [pl.BlockSpec((None, k, tn), lambda j, i: (layer, 0, j)),
                 pl.BlockSpec((tm, tn), lambda j, i: (i, j))]
    return pl.pallas_call(
        functools.partial(_out_proj_kernel, n_lhs=len(lhs)), name="out_proj",
        grid=(n // tn, m // tm),
        in_specs=in_specs,
        out_specs=pl.BlockSpec((tm, tn), lambda j, i: (i, j)),
        out_shape=jax.ShapeDtypeStruct((m, n), F32),
        scratch_shapes=[pltpu.VMEM((k, tn), BF16)],
        compiler_params=_cparams("arbitrary", "arbitrary"),
    )(*lhs, w, x)


def _rope_slab(r, c, s1, s2):
    return r * c + pltpu.roll(r, 96, 1) * s1 + pltpu.roll(r, 32, 1) * s2


def _q_prep_kernel(cq_ref, g_ref, w_ref, c_ref, s1_ref, s2_ref, gn_ref, gr_ref, q_ref):
    h = _rms(cq_ref[...], g_ref[...]).astype(BF16)
    c, s1, s2 = c_ref[...], s1_ref[...], s2_ref[...]
    scale = MLA_QK ** -0.5 * math.log2(math.e)
    gn = gn_ref[...] * scale
    gr = gr_ref[...] * scale
    for hd in range(MLA_HEADS):
        lo = hd * MLA_PAD
        nope = jnp.dot(h, w_ref[:, lo:lo + LANE], preferred_element_type=F32)
        rr = _rope_slab(jnp.dot(h, w_ref[:, lo + LANE:lo + MLA_PAD], preferred_element_type=F32), c, s1, s2)
        ss = jnp.sum(nope * nope, axis=-1, keepdims=True) + jnp.sum(rr * rr, axis=-1, keepdims=True)
        rs = lax.rsqrt(ss * (1.0 / MLA_QK) + EPS)
        q_ref[:, lo:lo + LANE] = (nope * rs * gn).astype(BF16)
        q_ref[:, lo + LANE:lo + MLA_PAD] = (rr * rs * gr).astype(BF16)


def mla_q_prep(p, g, w_uq, layer, tabs, gn, gr):
    m = p.shape[0]
    period = tabs[0].shape[0]
    tm = _pick(period, (256, 128))
    nb = period // tm
    tab_spec = pl.BlockSpec((tm, LANE), lambda i: (i % nb, 0))
    vec = lambda n: pl.BlockSpec((1, n), lambda i: (0, 0))
    return pl.pallas_call(
        _q_prep_kernel, name="mla_q_prep",
        grid=(m // tm,),
        in_specs=[pl.BlockSpec((tm, 768), lambda i: (i, 0)), vec(768),
                  pl.BlockSpec((None,) + w_uq.shape[1:], lambda i: (layer, 0, 0)),
                  tab_spec, tab_spec, tab_spec, vec(LANE), vec(LANE)],
        out_specs=pl.BlockSpec((tm, MLA_HEADS * MLA_PAD), lambda i: (i, 0)),
        out_shape=jax.ShapeDtypeStruct((m, MLA_HEADS * MLA_PAD), BF16),
        compiler_params=_cparams("parallel"),
    )(p, g.reshape(1, -1), w_uq, *tabs, gn, gr)


def _latent_kernel(ckv_a_ref, ckv_b_ref, kr_ref, g_ref, c_ref, s1_ref, s2_ref, lat_ref, krp_ref, kro_ref):
    lat_ref[...] = _rms(jnp.concatenate([ckv_a_ref[...], ckv_b_ref[...]], axis=-1), g_ref[...])
    rr = _rope_slab(kr_ref[...], c_ref[...], s1_ref[...], s2_ref[...])
    krp_ref[...] = rr
    kro_ref[...] = rr[:, :MLA_ROPE]


def mla_latent(p, g, tabs):
    m = p.shape[0]
    period = tabs[0].shape[0]
    tm = _pick(period, (512, 256, 128))
    nb = period // tm
    tab_spec = pl.BlockSpec((tm, LANE), lambda i: (i % nb, 0))
    return pl.pallas_call(
        _latent_kernel, name="mla_latent",
        grid=(m // tm,),
        in_specs=[pl.BlockSpec((tm, 256), lambda i: (i, AB_CKV // 256)),
                  pl.BlockSpec((tm, 256), lambda i: (i, AB_CKV // 256 + 1)),
                  pl.BlockSpec((tm, LANE), lambda i: (i, AB_KR // LANE)),
                  pl.BlockSpec((1, 512), lambda i: (0, 0)),
                  tab_spec, tab_spec, tab_spec],
        out_specs=[pl.BlockSpec((tm, 512), lambda i: (i, 0)),
                   pl.BlockSpec((tm, LANE), lambda i: (i, 0)),
                   pl.BlockSpec((tm, MLA_ROPE), lambda i: (i, 0))],
        out_shape=[jax.ShapeDtypeStruct((m, 512), F32),
                   jax.ShapeDtypeStruct((m, LANE), F32),
                   jax.ShapeDtypeStruct((m, MLA_ROPE), F32)],
        compiler_params=_cparams("parallel"),
    )(p, p, p, g.reshape(1, -1), *tabs)


def _kv_expand_kernel(lat_ref, krp_ref, wn_ref, wv_ref, gn_ref, gr_ref, k_ref, v_ref):
    lat = lat_ref[...].astype(BF16)
    kr = krp_ref[...]
    ss_r = jnp.sum(kr * kr, axis=-1, keepdims=True)
    gn, gr = gn_ref[...], gr_ref[...]
    for hd in range(MLA_HEADS):
        kn = jnp.dot(lat, wn_ref[:, hd * LANE:(hd + 1) * LANE], preferred_element_type=F32)
        rs = lax.rsqrt((jnp.sum(kn * kn, axis=-1, keepdims=True) + ss_r) * (1.0 / MLA_QK) + EPS)
        lo = hd * MLA_PAD
        k_ref[:, lo:lo + LANE] = (kn * rs * gn).astype(BF16)
        k_ref[:, lo + LANE:lo + MLA_PAD] = (kr * rs * gr).astype(BF16)
        v_ref[:, hd * LANE:(hd + 1) * LANE] = jnp.dot(
            lat, wv_ref[:, hd * LANE:(hd + 1) * LANE], preferred_element_type=F32).astype(BF16)


def mla_kv_expand(lat, krp, src_layer, wn, wv, layer, gn, gr):
    m = lat.shape[1]
    tm = _pick(m, (512, 256, 128))
    vec = pl.BlockSpec((1, LANE), lambda i: (0, 0))
    return pl.pallas_call(
        _kv_expand_kernel, name="mla_kv_expand",
        grid=(m // tm,),
        in_specs=[pl.BlockSpec((None, tm, 512), lambda i: (src_layer, i, 0)),
                  pl.BlockSpec((None, tm, LANE), lambda i: (src_layer, i, 0)),
                  pl.BlockSpec((None,) + wn.shape[1:], lambda i: (layer, 0, 0)),
                  pl.BlockSpec((None,) + wv.shape[1:], lambda i: (layer, 0, 0)), vec, vec],
        out_specs=[pl.BlockSpec((tm, MLA_HEADS * MLA_PAD), lambda i: (i, 0)),
                   pl.BlockSpec((tm, MLA_HEADS * MLA_V), lambda i: (i, 0))],
        out_shape=[jax.ShapeDtypeStruct((m, MLA_HEADS * MLA_PAD), BF16),
                   jax.ShapeDtypeStruct((m, MLA_HEADS * MLA_V), BF16)],
        compiler_params=_cparams("parallel"),
    )(lat, krp, wn, wv, gn, gr)


def _scores(q, kblk):
    return lax.dot_general(q, kblk, (((1,), (1,)), ((), ())), preferred_element_type=F32)


def _softmax_update(s, vblk, carry, mask=None):
    m, l, acc = carry
    if mask is not None:
        s = jnp.where(mask, s, NEG)
    m_new = jnp.maximum(m, jnp.max(s, axis=-1, keepdims=True))
    p = jnp.exp2(s - m_new)
    alpha = jnp.exp2(m - m_new)
    l = alpha * l + jnp.sum(p, axis=-1, keepdims=True)
    acc = alpha * acc + jnp.dot(p.astype(BF16), vblk, preferred_element_type=F32)
    return m_new, l, acc


def _softmax_step(q, kblk, vblk, carry, mask=None):
    return _softmax_update(_scores(q, kblk), vblk, carry, mask)


def _mla_attn_kernel(q_ref, k_ref, v_ref, g_ref, o_ref, *, tq):
    i = pl.program_id(2)
    hq = tq // 2
    qa, qb = q_ref[:hq, :], q_ref[hq:, :]

    def body(j, carry):
        off = pl.multiple_of(j * tq, tq)
        kblk, vblk = k_ref[pl.ds(off, tq), :], v_ref[pl.ds(off, tq), :]
        sa, sb = _scores(qa, kblk), _scores(qb, kblk)
        return _softmax_update(sa, vblk, carry[0]), _softmax_update(sb, vblk, carry[1])

    init = (jnp.full((hq, 1), NEG, F32), jnp.zeros((hq, 1), F32), jnp.zeros((hq, MLA_V), F32))
    ca, cb = lax.fori_loop(0, i, body, (init, init))
    off = pl.multiple_of(i * tq, tq)
    k_lo, v_lo = k_ref[pl.ds(off, hq), :], v_ref[pl.ds(off, hq), :]
    k_hi, v_hi = k_ref[pl.ds(off + hq, hq), :], v_ref[pl.ds(off + hq, hq), :]
    row = lax.broadcasted_iota(jnp.int32, (hq, hq), 0) // CHUNK
    col = lax.broadcasted_iota(jnp.int32, (hq, hq), 1) // CHUNK
    tri = col <= row
    sa, sb_lo, sb_hi = _scores(qa, k_lo), _scores(qb, k_lo), _scores(qb, k_hi)
    _, la, acca = _softmax_update(sa, v_lo, ca, tri)
    cb = _softmax_update(sb_lo, v_lo, cb)
    _, lb, accb = _softmax_update(sb_hi, v_hi, cb, tri)
    o_ref[:hq, :] = (acca / la * _silu(g_ref[:hq, :])).astype(o_ref.dtype)
    o_ref[hq:, :] = (accb / lb * _silu(g_ref[hq:, :])).astype(o_ref.dtype)


def mla_attn_prompt(q, k, v, p, *, bsz, seq):
    tq = _pick(seq, (512, 256))
    nq = seq // tq
    return pl.pallas_call(
        functools.partial(_mla_attn_kernel, tq=tq), name="mla_attn",
        grid=(bsz, MLA_HEADS, nq),
        in_specs=[pl.BlockSpec((tq, MLA_PAD), lambda b, h, i: (b * nq + i, h)),
                  pl.BlockSpec((seq, MLA_PAD), lambda b, h, i: (b, h)),
                  pl.BlockSpec((seq, MLA_V), lambda b, h, i: (b, h)),
                  pl.BlockSpec((tq, MLA_V), lambda b, h, i: (b * nq + i, AB_GA // MLA_V + h))],
        out_specs=pl.BlockSpec((tq, MLA_V), lambda b, h, i: (b * nq + i, h)),
        out_shape=jax.ShapeDtypeStruct((bsz * seq, MLA_HEADS * MLA_V), BF16),
        compiler_params=_cparams("parallel", "parallel", "arbitrary"),
    )(q, k, v, p)


def _mla_attn_step_kernel(q_ref, kc_ref, vc_ref, kn_ref, vn_ref, g_ref, o_ref, *, heads):
    t = q_ref.shape[0]
    init = (jnp.full((t, 1), NEG, F32), jnp.zeros((t, 1), F32), jnp.zeros((t, MLA_V), F32))
    for hd in range(heads):
        ks = slice(hd * MLA_PAD, (hd + 1) * MLA_PAD)
        vs = slice(hd * MLA_V, (hd + 1) * MLA_V)
        q = q_ref[:, ks]
        carry = _softmax_step(q, kc_ref[:, ks], vc_ref[:, vs], init)
        _, l, acc = _softmax_step(q, kn_ref[:, ks], vn_ref[:, vs], carry)
        o_ref[:, vs] = (acc / l * _silu(g_ref[:, vs])).astype(o_ref.dtype)


def mla_attn_step(q, kc, vc, kn, vn, p, *, bsz, t, past):
    heads = 4
    kw, vw = heads * MLA_PAD, heads * MLA_V
    return pl.pallas_call(
        functools.partial(_mla_attn_step_kernel, heads=heads), name="mla_attn_step",
        grid=(bsz, MLA_HEADS // heads),
        in_specs=[pl.BlockSpec((t, kw), lambda b, h: (b, h)),
                  pl.BlockSpec((past, kw), lambda b, h: (b, h)),
                  pl.BlockSpec((past, vw), lambda b, h: (b, h)),
                  pl.BlockSpec((t, kw), lambda b, h: (b, h)),
                  pl.BlockSpec((t, vw), lambda b, h: (b, h)),
                  pl.BlockSpec((t, vw), lambda b, h: (b, AB_GA // vw + h))],
        out_specs=pl.BlockSpec((t, vw), lambda b, h: (b, h)),
        out_shape=jax.ShapeDtypeStruct((bsz * t, MLA_HEADS * MLA_V), BF16),
        compiler_params=_cparams("parallel", "parallel"),
    )(q, kc, vc, kn, vn, p)


def _s5_disc_kernel(lre_ref, lim_ref, ldt_ref, bre_ref, bim_ref, lbre_ref, lbim_ref, bbre_ref, bbim_ref):
    lam_re = jnp.minimum(lre_ref[...], -1e-4)
    lam_im = lim_ref[...]
    dt = jnp.exp(ldt_ref[...])
    mag = jnp.exp(lam_re * dt)
    lb_re = mag * jnp.cos(lam_im * dt)
    lb_im = mag * jnp.sin(lam_im * dt)
    nr, ni = lb_re - 1.0, lb_im
    den = lam_re * lam_re + lam_im * lam_im
    f_re = (nr * lam_re + ni * lam_im) / den
    f_im = (ni * lam_re - nr * lam_im) / den
    b_re, b_im = bre_ref[...], bim_ref[...]
    lbre_ref[...] = lb_re
    lbim_ref[...] = lb_im
    bbre_ref[...] = f_re * b_re - f_im * b_im
    bbim_ref[...] = f_re * b_im + f_im * b_re


def s5_discretize(lam_re, lam_im, log_dt, b_re, b_im):
    g, pdim = lam_re.shape
    bt_re = jnp.swapaxes(b_re, 1, 2)
    bt_im = jnp.swapaxes(b_im, 1, 2)
    s3 = jax.ShapeDtypeStruct((g, 1, pdim), F32)
    sb = jax.ShapeDtypeStruct(bt_re.shape, F32)
    return pl.pallas_call(
        _s5_disc_kernel, name="s5_disc", out_shape=[s3, s3, sb, sb],
    )(lam_re.reshape(g, 1, pdim), lam_im.reshape(g, 1, pdim), log_dt.reshape(g, 1, 1), bt_re, bt_im)


def _s5_kernel(u_ref, h0re_ref, h0im_ref, wbu_ref, lbre_ref, lbim_ref, wcre_ref, wcim_ref, d_ref,
               y_ref, hre_ref, him_ref, ut_ref, sre_ref, sim_ref, *, bsz, tc, nsub, group, unroll):
    sub = 8
    assert bsz == sub and SSM_SUB * SSM_GROUP == LANE
    rows = bsz * tc
    sw = SSM_SUB * SSM_STATE
    uw = SSM_SUB * SSM_GROUP
    tiles = sw // LANE

    @pl.when(pl.program_id(1) == 0)
    def _():
        hre_ref[...] = h0re_ref[...]
        him_ref[...] = h0im_ref[...]

    for b in range(bsz):
        for s in range(nsub):
            ut_ref[s, pl.ds(b, tc, stride=bsz), :] = u_ref[b, :, s * uw:(s + 1) * uw]

    for s in range(nsub):
        bu = jnp.dot(ut_ref[s].astype(BF16), wbu_ref[s], preferred_element_type=F32)
        for k in range(tiles):
            sre_ref[s * tiles + k] = bu[:, k * LANE:(k + 1) * LANE]
            sim_ref[s * tiles + k] = bu[:, sw + k * LANE:sw + (k + 1) * LANE]

    for c0 in range(0, nsub * tiles, group):
        lanes = [slice((c0 + k) * LANE, (c0 + k + 1) * LANE) for k in range(group)]
        ar = [jnp.broadcast_to(lbre_ref[:, ls], (sub, LANE)) for ls in lanes]
        ai = [jnp.broadcast_to(lbim_ref[:, ls], (sub, LANE)) for ls in lanes]

        def step(i, carry, c0=c0, ar=ar, ai=ai):
            rs = pl.ds(pl.multiple_of(i * sub, sub), sub)
            out = []
            for k in range(group):
                hr, hi = carry[2 * k], carry[2 * k + 1]
                xr, xi = sre_ref[c0 + k, rs, :], sim_ref[c0 + k, rs, :]
                nr = ar[k] * hr - ai[k] * hi + xr
                ni = ar[k] * hi + ai[k] * hr + xi
                sre_ref[c0 + k, rs, :] = nr
                sim_ref[c0 + k, rs, :] = ni
                out += [nr, ni]
            return tuple(out)

        init = []
        for ls in lanes:
            init += [hre_ref[:, ls], him_ref[:, ls]]
        fin = lax.fori_loop(0, tc, step, tuple(init), unroll=unroll)
        for k, ls in enumerate(lanes):
            hre_ref[:, ls] = fin[2 * k]
            him_ref[:, ls] = fin[2 * k + 1]

    d = d_ref[...]
    for s in range(nsub):
        hre = jnp.concatenate([sre_ref[s * tiles + k] for k in range(tiles)], axis=-1).astype(BF16)
        him = jnp.concatenate([sim_ref[s * tiles + k] for k in range(tiles)], axis=-1).astype(BF16)
        y = jnp.dot(hre, wcre_ref[s], preferred_element_type=F32)
        y = y + jnp.dot(him, wcim_ref[s], preferred_element_type=F32)
        y = y + d[:, s * uw:(s + 1) * uw] * ut_ref[s]
        ut_ref[s] = _gelu_tanh(y)
    for b in range(bsz):
        for s in range(nsub):
            y_ref[b, :, s * uw:(s + 1) * uw] = ut_ref[s, pl.ds(b, tc, stride=bsz), :].astype(BF16)


def _s5_packed_kernel(u_ref, h0re_ref, h0im_ref, wbu_ref, lbre_ref, lbim_ref, wc_ref, d_ref,
                      y_ref, hre_ref, him_ref, ua_ref, st_ref, yt_ref, *, tc, nsub, group, unroll):
    bsz, sub = 4, 8
    rows = sub * tc
    sw = SSM_SUB * SSM_STATE
    uw = SSM_SUB * SSM_GROUP
    tiles = sw // LANE

    @pl.when(pl.program_id(1) == 0)
    def _():
        hre_ref[...] = h0re_ref[...]
        him_ref[...] = h0im_ref[...]
        ua_ref[...] = jnp.zeros(ua_ref.shape, F32)

    for b in range(bsz):
        for s in range(nsub):
            ub = u_ref[b, :, s * uw:(s + 1) * uw]
            ua_ref[s, 0, pl.ds(b, tc, stride=sub), :] = ub
            ua_ref[s, 1, pl.ds(bsz + b, tc, stride=sub), :] = ub

    for s in range(nsub):
        lhs = jnp.concatenate([ua_ref[s, 0], ua_ref[s, 1]], axis=-1).astype(BF16)
        bu = jnp.dot(lhs, wbu_ref[s], preferred_element_type=F32)
        for k in range(tiles):
            st_ref[s * tiles + k] = bu[:, k * LANE:(k + 1) * LANE]

    low = lax.broadcasted_iota(jnp.int32, (sub, LANE), 0) < bsz
    for c0 in range(0, nsub * tiles, group):
        lanes = [slice((c0 + k) * LANE, (c0 + k + 1) * LANE) for k in range(group)]
        a1 = [jnp.broadcast_to(lbre_ref[:, ls], (sub, LANE)) for ls in lanes]
        a2 = [jnp.where(low, -1.0, 1.0) * jnp.broadcast_to(lbim_ref[:, ls], (sub, LANE)) for ls in lanes]

        def step(t, carry, c0=c0, a1=a1, a2=a2):
            rs = pl.ds(pl.multiple_of(t * sub, sub), sub)
            out = []
            for k in range(group):
                h = carry[k]
                n = a1[k] * h + a2[k] * pltpu.roll(h, bsz, 0) + st_ref[c0 + k, rs, :]
                st_ref[c0 + k, rs, :] = n
                out.append(n)
            return tuple(out)

        init = tuple(jnp.concatenate([hre_ref[:, ls], him_ref[:, ls]], axis=0) for ls in lanes)
        fin = lax.fori_loop(0, tc, step, init, unroll=unroll)
        for k, ls in enumerate(lanes):
            hre_ref[:, ls] = fin[k][:bsz]
            him_ref[:, ls] = fin[k][bsz:]

    is_re = lax.broadcasted_iota(jnp.int32, (rows, uw), 0) % sub < bsz
    for s in range(nsub):
        hcat = jnp.concatenate([st_ref[s * tiles + k] for k in range(tiles)], axis=-1).astype(BF16)
        z = jnp.dot(hcat, wc_ref[s], preferred_element_type=F32)
        yt_ref[s] = jnp.where(is_re, z[:, :uw], z[:, uw:])
    d = d_ref[...]
    for b in range(bsz):
        for s in range(nsub):
            ls = slice(s * uw, (s + 1) * uw)
            y = yt_ref[s, pl.ds(b, tc, stride=sub), :] + yt_ref[s, pl.ds(bsz + b, tc, stride=sub), :]
            y_ref[b, :, ls] = _gelu_tanh(y + d[:, ls] * u_ref[b, :, ls]).astype(BF16)


def _s5_packed(p3, h0_re, h0_im, wbu, lb_re, lb_im, wc_re, wc_im, d_skip, *, tc):
    bsz, t, _ = p3.shape
    nsub = 4
    uw = nsub * SSM_SUB * SSM_GROUP
    sw = nsub * SSM_SUB * SSM_STATE
    nblk = wbu.shape[0] // nsub
    half = wbu.shape[2] // 2
    wbu2 = jnp.concatenate([wbu[:, :, :half], wbu[:, :, half:]], axis=1)
    wc2 = jnp.concatenate([wc_re, wc_im], axis=2)
    kern = functools.partial(_s5_packed_kernel, tc=tc, nsub=nsub, group=8, unroll=4)
    st_spec = pl.BlockSpec((bsz, sw), lambda g, c: (0, g))
    rows = 8 * tc
    return pl.pallas_call(
        kern, name="s5_mixer",
        grid=(nblk, t // tc),
        in_specs=[pl.BlockSpec((bsz, tc, uw), lambda g, c: (0, c, AB_U // uw + g)),
                  st_spec, st_spec,
                  pl.BlockSpec((nsub,) + wbu2.shape[1:], lambda g, c: (g, 0, 0)),
                  pl.BlockSpec((1, sw), lambda g, c: (0, g)),
                  pl.BlockSpec((1, sw), lambda g, c: (0, g)),
                  pl.BlockSpec((nsub,) + wc2.shape[1:], lambda g, c: (g, 0, 0)),
                  pl.BlockSpec((1, uw), lambda g, c: (0, g))],
        out_specs=[pl.BlockSpec((bsz, tc, uw), lambda g, c: (0, c, g)), st_spec, st_spec],
        out_shape=[jax.ShapeDtypeStruct((bsz, t, uw * nblk), BF16),
                   jax.ShapeDtypeStruct(h0_re.shape, F32),
                   jax.ShapeDtypeStruct(h0_im.shape, F32)],
        scratch_shapes=[pltpu.VMEM((nsub, 2, rows, LANE), F32),
                        pltpu.VMEM((sw // LANE, rows, LANE), F32),
                        pltpu.VMEM((nsub, rows, LANE), F32)],
        compiler_params=_cparams("parallel", "arbitrary"),
    )(p3, h0_re, h0_im, wbu2, lb_re, lb_im, wc2, d_skip)


def s5_mixer(p3, h0_re, h0_im, wbu, lb_re, lb_im, wc_re, wc_im, d_skip, *, tc):
    bsz, t, _ = p3.shape
    if bsz == 4:
        return _s5_packed(p3, h0_re, h0_im, wbu, lb_re, lb_im, wc_re, wc_im, d_skip, tc=tc)
    nsub = 4
    uw = nsub * SSM_SUB * SSM_GROUP
    sw = nsub * SSM_SUB * SSM_STATE
    nblk = wbu.shape[0] // nsub
    bw = uw * nblk
    kern = functools.partial(_s5_kernel, bsz=bsz, tc=tc, nsub=nsub, group=4, unroll=4)
    st_spec = pl.BlockSpec((bsz, sw), lambda g, c: (0, g))
    return pl.pallas_call(
        kern, name="s5_mixer",
        grid=(nblk, t // tc),
        in_specs=[pl.BlockSpec((bsz, tc, uw), lambda g, c: (0, c, AB_U // uw + g)),
                  st_spec, st_spec,
                  pl.BlockSpec((nsub,) + wbu.shape[1:], lambda g, c: (g, 0, 0)),
                  pl.BlockSpec((1, sw), lambda g, c: (0, g)),
                  pl.BlockSpec((1, sw), lambda g, c: (0, g)),
                  pl.BlockSpec((nsub,) + wc_re.shape[1:], lambda g, c: (g, 0, 0)),
                  pl.BlockSpec((nsub,) + wc_im.shape[1:], lambda g, c: (g, 0, 0)),
                  pl.BlockSpec((1, uw), lambda g, c: (0, g))],
        out_specs=[pl.BlockSpec((bsz, tc, uw), lambda g, c: (0, c, g)), st_spec, st_spec],
        out_shape=[jax.ShapeDtypeStruct((bsz, t, bw), BF16),
                   jax.ShapeDtypeStruct(h0_re.shape, F32),
                   jax.ShapeDtypeStruct(h0_im.shape, F32)],
        scratch_shapes=[pltpu.VMEM((nsub, bsz * tc, LANE), F32),
                        pltpu.VMEM((sw // LANE, bsz * tc, LANE), F32),
                        pltpu.VMEM((sw // LANE, bsz * tc, LANE), F32)],
        compiler_params=_cparams("parallel", "arbitrary"),
    )(p3, h0_re, h0_im, wbu, lb_re, lb_im, wc_re, wc_im, d_skip)


def _glu_kernel(y_ref, wa32_ref, wb32_ref, g_ref, o_ref, wa_ref, wb_ref):
    @pl.when(pl.program_id(1) == 0)
    def _():
        wa_ref[...] = wa32_ref[...].astype(BF16)
        wb_ref[...] = wb32_ref[...].astype(BF16)

    y = y_ref[...]
    ga = jnp.dot(y, wa_ref[...], preferred_element_type=F32)
    gb = jnp.dot(y, wb_ref[...], preferred_element_type=F32)
    o_ref[...] = (ga * jax.nn.sigmoid(gb) * _silu(g_ref[...])).astype(o_ref.dtype)


def glu_gate(y, w_glu, layer, p, *, tn=512):
    m, k = y.shape
    n = w_glu.shape[2] // 2
    tm = _pick(m, (1024, 512, 256, 128))
    nj = n // tn
    return pl.pallas_call(
        _glu_kernel, name="glu_gate",
        grid=(nj, m // tm),
        in_specs=[pl.BlockSpec((tm, k), lambda j, i: (i, 0)),
                  pl.BlockSpec((None, k, tn), lambda j, i: (layer, 0, j)),
                  pl.BlockSpec((None, k, tn), lambda j, i: (layer, 0, nj + j)),
                  pl.BlockSpec((tm, tn), lambda j, i: (i, AB_GB // tn + j))],
        out_specs=pl.BlockSpec((tm, tn), lambda j, i: (i, j)),
        out_shape=jax.ShapeDtypeStruct((m, n), BF16),
        scratch_shapes=[pltpu.VMEM((k, tn), BF16), pltpu.VMEM((k, tn), BF16)],
        compiler_params=_cparams("arbitrary", "arbitrary"),
    )(y, w_glu, w_glu, p)


C_Q = 0
C_G = 4096
C_K = 8192
C_V = 8704
KVW = SWA_KV_HEADS * SWA_HEAD_DIM


def _k_norm_kernel(k_ref, g_ref, o_ref):
    k = k_ref[...]
    g = g_ref[...]
    for h in range(SWA_KV_HEADS):
        sl = slice(h * SWA_HEAD_DIM, (h + 1) * SWA_HEAD_DIM)
        o_ref[:, sl] = _rms(k[:, sl], g)


def swa_k_norm(pc, gk):
    m = pc.shape[0]
    tm = _pick(m, (512, 256, 128))
    return pl.pallas_call(
        _k_norm_kernel, name="swa_k_norm",
        grid=(m // tm,),
        in_specs=[pl.BlockSpec((tm, KVW), lambda i: (i, C_K // KVW)),
                  pl.BlockSpec((1, SWA_HEAD_DIM), lambda i: (0, 0))],
        out_specs=pl.BlockSpec((tm, KVW), lambda i: (i, 0)),
        out_shape=jax.ShapeDtypeStruct((m, KVW), F32),
        compiler_params=_cparams("parallel"),
    )(pc, gk.reshape(1, -1))


def _swa_prompt_kernel(sink_ref, q_ref, kp_ref, kc_ref, vp_ref, vc_ref, g_ref, gq_ref, b_ref, o_ref, *, pair_batch):
    nk = b_ref.shape[-1]
    pad = nk - kp_ref.shape[0] - kc_ref.shape[0]
    fill = [jnp.zeros((pad, KVW), F32)] if pad else []
    kcat = jnp.concatenate([kp_ref[...], kc_ref[...]] + fill, axis=0)
    vcat = jnp.concatenate([vp_ref[...], vc_ref[...]] + fill, axis=0)
    seg_r = lax.broadcasted_iota(jnp.int32, (LANE, LANE), 0) // SWA_HEAD_DIM
    seg_c = lax.broadcasted_iota(jnp.int32, (LANE, LANE), 1) // SWA_HEAD_DIM
    seg_ones = (seg_r == seg_c).astype(BF16)
    left = lax.broadcasted_iota(jnp.int32, (1, LANE), 1) < SWA_HEAD_DIM
    gq = gq_ref[...] * (SWA_HEAD_DIM ** -0.5 * LOG2E)
    nt = (((1,), (1,)), ((), ()))

    def halves(slab, half):
        own = jnp.where(left if half == 0 else jnp.logical_not(left), slab, 0.0)
        other = pltpu.roll(own, SWA_HEAD_DIM, 1)
        lo, hi = (own, other) if half == 0 else (other, own)
        return lo.astype(BF16), hi.astype(BF16)

    kv = {}

    def kv_halves(kvh):
        if kvh not in kv:
            ks = slice((kvh // 2) * LANE, (kvh // 2 + 1) * LANE)
            kv.clear()
            kv[kvh] = halves(kcat[:, ks], kvh % 2) + halves(vcat[:, ks], kvh % 2)
        return kv[kvh]

    def scores(pair):
        sl = slice(pair * LANE, (pair + 1) * LANE)
        k_lo, k_hi, _, _ = kv_halves(pair // (SWA_GQ // 2))
        q = q_ref[:, sl]
        x2 = q * q
        x2_hi = x2.astype(BF16)
        x2_lo = (x2 - x2_hi.astype(F32)).astype(BF16)
        ss = (jnp.dot(x2_hi, seg_ones, preferred_element_type=F32)
              + jnp.dot(x2_lo, seg_ones, preferred_element_type=F32))
        qn = (q * lax.rsqrt(ss * (1.0 / SWA_HEAD_DIM) + EPS) * gq).astype(BF16)
        return [lax.dot_general(qn, k_lo, nt, preferred_element_type=F32) + b_ref[2 * pair],
                lax.dot_general(qn, k_hi, nt, preferred_element_type=F32) + b_ref[2 * pair + 1]]

    def finish(pairs, ss, vhs):
        sinks = [sink_ref[2 * p + j] * LOG2E for p in pairs for j in range(2)]
        flat = [s for pr in ss for s in pr]
        ms = [jnp.maximum(jnp.max(s, axis=-1, keepdims=True), sk) for s, sk in zip(flat, sinks)]
        es = [jnp.exp2(s - m) for s, m in zip(flat, ms)]
        ds = [jnp.sum(e, axis=-1, keepdims=True) + jnp.exp2(sk - m) for e, m, sk in zip(es, ms, sinks)]
        for i, p in enumerate(pairs):
            v_lo, v_hi = vhs[i]
            o = (jnp.dot(es[2 * i].astype(BF16), v_lo, preferred_element_type=F32)
                 + jnp.dot(es[2 * i + 1].astype(BF16), v_hi, preferred_element_type=F32))
            inv = jnp.where(left, 1.0 / ds[2 * i], 1.0 / ds[2 * i + 1])
            sl = slice(p * LANE, (p + 1) * LANE)
            o_ref[:, sl] = (o * inv * _silu(g_ref[:, sl])).astype(o_ref.dtype)

    npairs = SWA_HEADS // 2
    batches = [list(range(b, b + pair_batch)) for b in range(0, npairs, pair_batch)]
    nxt = ([scores(p) for p in batches[0]], [kv_halves(p // (SWA_GQ // 2))[2:] for p in batches[0]])
    for bi, pairs in enumerate(batches):
        cur = nxt
        if bi + 1 < len(batches):
            nb = batches[bi + 1]
            nxt = ([scores(p) for p in nb], [kv_halves(p // (SWA_GQ // 2))[2:] for p in nb])
        finish(pairs, *cur)


def swa_attn_prompt(pc, kn, gq, sinks, bias, *, bsz, seq):
    tq = WINDOW
    nq = seq // tq
    dm = SWA_HEADS * SWA_HEAD_DIM
    prev = lambda b, i: (b * nq + jnp.maximum(i - 1, 0), 0)
    cur = lambda b, i: (b * nq + i, 0)
    return pl.pallas_call(
        functools.partial(_swa_prompt_kernel, pair_batch=2), name="swa_attn",
        grid=(bsz, nq),
        in_specs=[pl.BlockSpec(memory_space=pltpu.SMEM),
                  pl.BlockSpec((tq, dm), lambda b, i: (b * nq + i, C_Q // dm)),
                  pl.BlockSpec((tq, KVW), prev), pl.BlockSpec((tq, KVW), cur),
                  pl.BlockSpec((tq, KVW), lambda b, i: (b * nq + jnp.maximum(i - 1, 0), C_V // KVW)),
                  pl.BlockSpec((tq, KVW), lambda b, i: (b * nq + i, C_V // KVW)),
                  pl.BlockSpec((tq, dm), lambda b, i: (b * nq + i, C_G // dm)),
                  pl.BlockSpec((1, LANE), lambda b, i: (0, 0)),
                  pl.BlockSpec((None,) + bias.shape[1:], lambda b, i: (jnp.minimum(i, 1), 0, 0, 0))],
        out_specs=pl.BlockSpec((tq, dm), cur),
        out_shape=jax.ShapeDtypeStruct((bsz * seq, dm), BF16),
        compiler_params=_cparams("parallel", "arbitrary"),
    )(sinks, pc, kn, kn, pc, pc, pc, jnp.tile(gq.reshape(1, -1), (1, 2)), bias)


def swa_attn_step(pc, k_cache, kn, v_cache, layer, gq, sinks, bias, *, bsz, t):
    dm = SWA_HEADS * SWA_HEAD_DIM
    row = lambda b: (b, 0)
    cache = pl.BlockSpec((None, WINDOW, KVW), lambda b: (layer, b, 0))
    return pl.pallas_call(
        functools.partial(_swa_prompt_kernel, pair_batch=4), name="swa_attn_step",
        grid=(bsz,),
        in_specs=[pl.BlockSpec(memory_space=pltpu.SMEM),
                  pl.BlockSpec((t, dm), lambda b: (b, C_Q // dm)),
                  cache, pl.BlockSpec((t, KVW), row), cache,
                  pl.BlockSpec((t, KVW), lambda b: (b, C_V // KVW)),
                  pl.BlockSpec((t, dm), lambda b: (b, C_G // dm)),
                  pl.BlockSpec((1, LANE), lambda b: (0, 0)),
                  pl.BlockSpec(bias.shape, lambda b: (0, 0, 0))],
        out_specs=pl.BlockSpec((t, dm), row),
        out_shape=jax.ShapeDtypeStruct((bsz * t, dm), BF16),
        compiler_params=_cparams("parallel"),
    )(sinks, pc, k_cache, kn, v_cache, pc, pc, jnp.tile(gq.reshape(1, -1), (1, 2)), bias)


def _rope_tables(pos, reps=1):
    half = MLA_ROPE // 2
    inv = ROPE_THETA ** (-jnp.arange(half, dtype=F32) / half)
    ang = pos.astype(F32)[:, None] * inv
    cos, sin = jnp.cos(ang), jnp.sin(ang)
    z = jnp.zeros_like(cos)
    tabs = (jnp.concatenate([cos, cos, z, z], -1), jnp.concatenate([-sin, z, z, z], -1),
            jnp.concatenate([z, sin, z, z], -1))
    return tuple(jnp.tile(t, (reps, 1)) for t in tabs)


def _rel_bucket(rel):
    nb = REL_BUCKETS // 2
    max_exact = nb // 2
    ret = jnp.where(rel > 0, nb, 0)
    n = jnp.abs(rel)
    nf = jnp.maximum(n, 1).astype(F32)
    large = max_exact + (jnp.log(nf / max_exact) / math.log(REL_MAX_DIST / max_exact)
                         * (nb - max_exact)).astype(jnp.int32)
    large = jnp.minimum(large, nb - 1)
    return ret + jnp.where(n < max_exact, n, large)


def _rel_bias(table, rel, mask):
    onehot = (_rel_bucket(rel)[:, :, None] == jnp.arange(REL_BUCKETS)).astype(F32)
    b = jnp.einsum("qkb,bh->hqk", onehot, table.astype(F32) * LOG2E, precision=lax.Precision.HIGHEST)
    return jnp.where(mask[None], b, NEG)


def _prep_ab_weights(w_in, w_uq, w_ukv):
    nl = w_in.shape[0]
    w_in_r = jnp.swapaxes(w_in, 1, 2)
    uq = w_uq.astype(BF16).reshape(nl, w_uq.shape[1], MLA_HEADS, MLA_QK)
    uq = jnp.pad(uq, ((0, 0), (0, 0), (0, 0), (0, MLA_PAD - MLA_QK))).reshape(nl, w_uq.shape[1], -1)
    ukv = w_ukv.astype(BF16).reshape(nl, w_ukv.shape[1], MLA_HEADS, MLA_NOPE + MLA_V)
    wn = ukv[:, :, :, :MLA_NOPE].reshape(nl, w_ukv.shape[1], -1)
    wv = ukv[:, :, :, MLA_NOPE:].reshape(nl, w_ukv.shape[1], -1)
    return dict(w_in=w_in_r, w_uq=uq, wn=wn, wv=wv)


def _head_gains(g):
    return g[:MLA_NOPE].reshape(1, LANE), jnp.pad(g[MLA_NOPE:], (0, LANE - MLA_ROPE)).reshape(1, LANE)


def _prep_s5(lam_re, lam_im, log_dt, b_re, b_im, c_re, c_im):
    g = lam_re.shape[0]
    lb_re, lb_im, bbt_re, bbt_im = s5_discretize(lam_re, lam_im, log_dt, b_re, b_im)
    eye = jnp.eye(SSM_SUB, dtype=F32)
    ns = g // SSM_SUB
    bb = jnp.stack([bbt_re, bbt_im]).reshape(2, ns, SSM_SUB, SSM_GROUP, SSM_STATE)
    wbu = bb[:, :, :, :, None, :] * eye[None, None, :, None, :, None]
    wbu = jnp.transpose(wbu, (1, 2, 3, 0, 4, 5)).reshape(ns, SSM_SUB * SSM_GROUP, 2 * SSM_SUB * SSM_STATE)

    def readout(c):
        c4 = c.astype(F32).reshape(ns, SSM_SUB, SSM_GROUP, SSM_STATE)
        w = c4[:, :, :, None, :] * eye[None, :, None, :, None]
        return jnp.transpose(w, (0, 3, 4, 1, 2)).reshape(ns, SSM_SUB * SSM_STATE, SSM_SUB * SSM_GROUP)

    return dict(wbu=wbu.astype(BF16), lb_re=lb_re.reshape(1, -1), lb_im=lb_im.reshape(1, -1),
                wc_re=readout(c_re).astype(BF16), wc_im=readout(-c_im).astype(BF16))


def _ab_layer(x, i, w, prm, s5w, tabs, *, bsz, seq, past):
    m = x.shape[0]
    gqn, gqr = _head_gains(prm["q_norm"][i])
    gkn, gkr = _head_gains(prm["k_norm"][i])
    p = ab_in_proj(rms_cast(x, prm["norm"][i]), w["w_in"], i)
    q = mla_q_prep(p, prm["q_lora_norm"][i], w["w_uq"], i, tabs, gqn, gqr)
    lat, krp, kro = mla_latent(p, prm["kv_lora_norm"][i], tabs)
    k, v = mla_kv_expand(lat[None], krp[None], 0, w["wn"], w["wv"], i, gkn, gkr)
    p3 = p.reshape(bsz, seq, AB_COLS)
    gdim = s5w["lb_re"].shape[1]
    d_skip = prm["d_skip"][i].reshape(1, -1)
    s5args = (s5w["wbu"], s5w["lb_re"], s5w["lb_im"], s5w["wc_re"], s5w["wc_im"], d_skip)
    if past is None:
        a_out = mla_attn_prompt(q, k, v, p, bsz=bsz, seq=seq)
        h0 = jnp.zeros((bsz, gdim), F32)
        y, h_re, h_im = s5_mixer(p3, h0, h0, *s5args, tc=2 * CHUNK)
    else:
        lat_c, kr_c, h0_re, h0_im = past
        kc, vc = mla_kv_expand(lat_c, kr_c, i, w["wn"], w["wv"], i, gkn, gkr)
        a_out = mla_attn_step(q, kc, vc, k, v, p, bsz=bsz, t=seq, past=lat_c.shape[1] // bsz)
        y, h_re, h_im = s5_mixer(p3, h0_re.reshape(bsz, gdim), h0_im.reshape(bsz, gdim), *s5args, tc=seq)
    b_out = glu_gate(y.reshape(m, -1), prm["w_glu"], i, p)
    x_new = out_proj([a_out, b_out], prm["w_out"], i, x)
    return x_new, (lat, kro, h_re, h_im)


def _c_in_col(j):
    return jnp.where(j < 8, j, jnp.where(j < 10, j + 8, j - 2))


def _c_layer(x, i, prm, biases, *, bsz, seq, past):
    assert prm["w_in"].shape[2] == C_V + KVW and C_G == 8 * KVW and C_K == 16 * KVW
    pc = wcast_matmul(rms_cast(x, prm["norm"][i]), prm["w_in"], i, _c_in_col, tn=KVW)
    gq, gk, sinks = prm["q_norm"][i], prm["k_norm"][i], prm["sinks"][i]
    kn = swa_k_norm(pc, gk)
    v = pc[:, C_V:C_V + KVW]
    if past is None:
        a = swa_attn_prompt(pc, kn, gq, sinks, *biases, bsz=bsz, seq=seq)
        k_buf = kn.reshape(bsz, seq, KVW)[:, -WINDOW:]
        v_buf = v.reshape(bsz, seq, KVW)[:, -WINDOW:]
    else:
        k_c, v_c = past
        a = swa_attn_step(pc, k_c, kn, v_c, i, gq, sinks, *biases, bsz=bsz, t=seq)
        k_buf = jnp.concatenate([k_c[i].reshape(bsz, WINDOW, KVW), kn.reshape(bsz, seq, KVW)], axis=1)[:, -WINDOW:]
        v_buf = jnp.concatenate([v_c[i].reshape(bsz, WINDOW, KVW), v.reshape(bsz, seq, KVW)], axis=1)[:, -WINDOW:]
    x_new = out_proj([a], prm["w_out"], i, x)
    shp = (bsz, WINDOW, SWA_KV_HEADS, SWA_HEAD_DIM)
    return x_new, (k_buf.reshape(shp), v_buf.reshape(shp))


def kernel(x_prompt, x_sample, cache_mla_latent, cache_mla_krope, state_ssm_re, state_ssm_im, cache_swa_k, cache_swa_v, rel_bias_table, ab_norm, ab_w_in, ab_q_lora_norm, ab_kv_lora_norm, ab_w_uq, ab_w_ukv, ab_q_norm, ab_k_norm, ssm_lambda_re, ssm_lambda_im, ssm_b_re, ssm_b_im, ssm_c_re, ssm_c_im, ssm_log_dt, ssm_d, ssm_w_glu, ab_w_out, c_norm, c_w_in, c_q_norm, c_k_norm, c_sinks, c_w_out):
    bp, lp, d = x_prompt.shape
    bs, ts, _ = x_sample.shape
    past_len = cache_mla_latent.shape[2]
    depth = ab_norm.shape[0] + c_norm.shape[0]
    assert lp % 256 == 0 and (bs * ts) % 128 == 0 and past_len % 512 == 0 and d == SWA_HEADS * SWA_HEAD_DIM

    q_pos = past_len + np.arange(ts)
    assert np.all((np.arange(past_len + ts)[None, :] // CHUNK) <= (q_pos[:, None] // CHUNK))
    k_pos = past_len - WINDOW + np.arange(WINDOW + ts)
    qch, kch = q_pos[:, None] // CHUNK, k_pos[None, :] // CHUNK
    assert np.all((kch <= qch) & (kch >= qch - WINDOW // CHUNK))

    tabs_p = _rope_tables(jnp.arange(lp))
    tabs_s = _rope_tables(past_len + jnp.arange(ts), reps=bs)

    qi = np.arange(WINDOW)[:, None]
    kj = np.arange(2 * WINDOW)[None, :]
    band = (kj // CHUNK - 2 <= qi // CHUNK) & (kj // CHUNK >= qi // CHUNK)
    bias_p = _rel_bias(rel_bias_table, jnp.asarray(kj - WINDOW - qi), jnp.asarray(band))
    biases_p = (jnp.stack([jnp.where(jnp.asarray(kj < WINDOW)[None], NEG, bias_p), bias_p]),)
    kj_s = np.arange(2 * WINDOW)[None, :]
    rel_s = np.where(kj_s < WINDOW + ts, (past_len - WINDOW + kj_s) - q_pos[:, None], 0)
    biases_s = (_rel_bias(rel_bias_table, jnp.asarray(rel_s),
                          jnp.asarray(np.broadcast_to(kj_s < WINDOW + ts, rel_s.shape))),)

    xp = x_prompt.reshape(bp * lp, d)
    xs = x_sample.reshape(bs * ts, d)
    outs_p = [[] for _ in range(6)]
    outs_s = [[] for _ in range(6)]
    n_ab = ab_norm.shape[0]
    ab_w = _prep_ab_weights(ab_w_in, ab_w_uq, ab_w_ukv)
    ab_prm = dict(norm=ab_norm, q_lora_norm=ab_q_lora_norm, kv_lora_norm=ab_kv_lora_norm, q_norm=ab_q_norm,
                  k_norm=ab_k_norm, d_skip=ssm_d, w_glu=ssm_w_glu, w_out=ab_w_out)
    c_prm = dict(norm=c_norm, w_in=c_w_in, q_norm=c_q_norm, k_norm=c_k_norm, sinks=c_sinks, w_out=c_w_out)
    lat_cache = cache_mla_latent.reshape(n_ab, bs * past_len, -1)
    kr_cache = jnp.pad(cache_mla_krope.reshape(n_ab, bs * past_len, -1), ((0, 0), (0, 0), (0, LANE - MLA_ROPE)))
    swa_past = (cache_swa_k.reshape(-1, bs * WINDOW, KVW), cache_swa_v.reshape(-1, bs * WINDOW, KVW))
    for layer in range(depth):
        i = layer // 2
        if layer % 2 == 0:
            s5w = _prep_s5(ssm_lambda_re[i], ssm_lambda_im[i], ssm_log_dt[i], ssm_b_re[i], ssm_b_im[i],
                           ssm_c_re[i], ssm_c_im[i])
            xp, (a, b, c, e) = _ab_layer(xp, i, ab_w, ab_prm, s5w, tabs_p, bsz=bp, seq=lp, past=None)
            outs_p[0].append(a.reshape(bp, lp, -1)); outs_p[1].append(b.reshape(bp, lp, -1))
            outs_p[2].append(c.reshape(bp, -1, SSM_STATE)); outs_p[3].append(e.reshape(bp, -1, SSM_STATE))
            past = (lat_cache, kr_cache, state_ssm_re[i], state_ssm_im[i])
            xs, (a, b, c, e) = _ab_layer(xs, i, ab_w, ab_prm, s5w, tabs_s, bsz=bs, seq=ts, past=past)
            outs_s[0].append(a.reshape(bs, ts, -1)); outs_s[1].append(b.reshape(bs, ts, -1))
            outs_s[2].append(c.reshape(bs, -1, SSM_STATE)); outs_s[3].append(e.reshape(bs, -1, SSM_STATE))
        else:
            xp, (a, b) = _c_layer(xp, i, c_prm, biases_p, bsz=bp, seq=lp, past=None)
            outs_p[4].append(a); outs_p[5].append(b)
            xs, (a, b) = _c_layer(xs, i, c_prm, biases_s, bsz=bs, seq=ts, past=swa_past)
            outs_s[4].append(a); outs_s[5].append(b)
    return (xp.reshape(bp, lp, d), xs.reshape(bs, ts, d),
            *[jnp.stack(o) for o in outs_p], *[jnp.stack(o) for o in outs_s])
```
